```python
import jax, jax.numpy as jnp
from jax import lax
import numpy as np

D_MODEL = 2048
BATCH = 16
SEQ = 256
DEPTH = 1
DEC_BATCH = 4
DEC_SEQ = 4096
PAST_LEN = 512

GRID_W = 64
HEAD_DIM = 128
N_Q_HEADS = 8
N_KV_HEADS = 2
Q_PER_KV = N_Q_HEADS // N_KV_HEADS
ATTN_WIDTH = N_Q_HEADS * HEAD_DIM
KV_WIDTH = N_KV_HEADS * HEAD_DIM
POOL_WINDOWS = (2, 4, 8, 16)
N_POOL_GROUPS = len(POOL_WINDOWS)
POOL_WIDTH = D_MODEL - ATTN_WIDTH
POOL_GROUP_WIDTH = POOL_WIDTH // N_POOL_GROUPS
MIX_WIDTH = ATTN_WIDTH + POOL_WIDTH
IN_WIDTH = ATTN_WIDTH + 2 * KV_WIDTH + POOL_WIDTH
D_FF = 4 * D_MODEL
Q_BLOCK = 128
ROPE_THETA = 10000.0
ROPE_QUARTER = HEAD_DIM // 4
EPS = 1e-6
N_MOD = 6
DEEPNORM_ALPHA = (2.0 * DEPTH) ** 0.25
DEEPNORM_BETA = (8.0 * DEPTH) ** -0.25

kernel_name = "hybrid_pool_gqa_prefix_diffusion_step"


def _layer_norm(x):
    xf = x.astype(jnp.float32)
    mu = jnp.mean(xf, axis=-1, keepdims=True)
    var = jnp.mean(jnp.square(xf - mu), axis=-1, keepdims=True)
    return (xf - mu) * lax.rsqrt(var + EPS)


def _rms_norm(x, gain):
    xf = x.astype(jnp.float32)
    return xf * lax.rsqrt(jnp.mean(jnp.square(xf), axis=-1, keepdims=True) + EPS) * gain.astype(jnp.float32)


def _rotate(x, cos, sin):
    x1, x2 = jnp.split(x, 2, axis=-1)
    return jnp.concatenate([x1 * cos - x2 * sin, x2 * cos + x1 * sin], axis=-1)


def _apply_rope(x, rope):
    cos_r, sin_r, cos_c, sin_c = rope
    xr, xc = jnp.split(x.astype(jnp.float32), 2, axis=-1)
    return jnp.concatenate([_rotate(xr, cos_r, sin_r), _rotate(xc, cos_c, sin_c)], axis=-1)


def _grid_rope(n_tokens):
    n_rows = n_tokens // GRID_W
    row = jnp.repeat(jnp.arange(n_rows, dtype=jnp.float32), GRID_W)
    col = jnp.tile(jnp.arange(GRID_W, dtype=jnp.float32), n_rows)
    inv_freq = ROPE_THETA ** (-(jnp.arange(ROPE_QUARTER, dtype=jnp.float32) / ROPE_QUARTER))
    ang_r = (row[:, None] * inv_freq)[:, None, :]
    ang_c = (col[:, None] * inv_freq)[:, None, :]
    return (jnp.cos(ang_r), jnp.sin(ang_r), jnp.cos(ang_c), jnp.sin(ang_c))


def _block_attention(q, k, v):
    b, t = q.shape[0], q.shape[1]
    nb = t // Q_BLOCK
    qb = q.reshape(b, nb, Q_BLOCK, N_KV_HEADS, Q_PER_KV, HEAD_DIM).transpose(1, 0, 2, 3, 4, 5)
    kf = k.astype(jnp.float32)
    vf = v.astype(jnp.float32)
    scale = HEAD_DIM ** -0.5

    def one_block(q_blk):
        s = jnp.einsum('bqkgd,bskd->bkgqs', q_blk.astype(jnp.float32), kf) * scale
        p = jax.nn.softmax(s, axis=-1)
        return jnp.einsum('bkgqs,bskd->bqkgd', p, vf)

    out = lax.map(one_block, qb)
    return out.transpose(1, 0, 2, 3, 4, 5).reshape(b, t, ATTN_WIDTH)


def _pool_mixer(p, w_pool, pool_scale):
    b, t, _ = p.shape
    pg = p.reshape(b, t, N_POOL_GROUPS, POOL_GROUP_WIDTH).astype(jnp.float32)
    csum = jnp.concatenate([jnp.zeros((b, 1, N_POOL_GROUPS, POOL_GROUP_WIDTH), jnp.float32),
                            jnp.cumsum(pg, axis=1)], axis=1)
    pos = jnp.arange(t)
    outs = []
    for g, w in enumerate(POOL_WINDOWS):
        lo = jnp.clip(pos - w // 2, 0, t)
        hi = jnp.clip(pos + w // 2, 0, t)
        cg = csum[:, :, g]
        s = jnp.take(cg, hi, axis=1) - jnp.take(cg, lo, axis=1)
        cnt = (hi - lo).astype(jnp.float32)[None, :, None]
        outs.append(s / cnt - pg[:, :, g])
    pooled = jnp.stack(outs, axis=2)
    mixed = jnp.einsum('btgc,gce->btge', pooled, w_pool.astype(jnp.float32))
    return (mixed.reshape(b, t, POOL_WIDTH) * pool_scale.astype(jnp.float32)).astype(p.dtype)


def _layer(x, mod, w_in, q_gain, k_gain, w_pool, pool_scale, w_out,
           ln1_g, ln1_b, w_ff1, w_ff2, ln2_g, ln2_b, rope=None, k_ctx=None, v_ctx=None):
    dt = x.dtype
    b, t, _ = x.shape
    shift1, scale1, gate1 = mod[:, None, 0], mod[:, None, 1], mod[:, None, 2]
    shift2, scale2, gate2 = mod[:, None, 3], mod[:, None, 4], mod[:, None, 5]

    u = (_layer_norm(x) * (1.0 + scale1) + shift1).astype(dt)
    proj = jnp.einsum('btd,de->bte', u, w_in)
    q_raw, k_raw, v_raw, p = jnp.split(proj, [ATTN_WIDTH, ATTN_WIDTH + KV_WIDTH, ATTN_WIDTH + 2 * KV_WIDTH], axis=-1)
    q = _rms_norm(q_raw.reshape(b, t, N_Q_HEADS, HEAD_DIM), q_gain)
    k = _rms_norm(k_raw.reshape(b, t, N_KV_HEADS, HEAD_DIM), k_gain)
    v = v_raw.reshape(b, t, N_KV_HEADS, HEAD_DIM)
    if rope is not None:
        q = _apply_rope(q, rope)
        k = _apply_rope(k, rope)
    q = q.astype(dt)
    k = k.astype(dt)
    if k_ctx is not None:
        keys = jnp.concatenate([k_ctx.astype(dt), k], axis=1)
        vals = jnp.concatenate([v_ctx.astype(dt), v], axis=1)
    else:
        keys, vals = k, v
    attn = _block_attention(q, keys, vals).astype(dt)
    pool = _pool_mixer(p, w_pool, pool_scale)
    mix = jnp.einsum('bte,ed->btd', jnp.concatenate([attn, pool], axis=-1), w_out)
    x1 = (_layer_norm(DEEPNORM_ALPHA * x + gate1 * mix) * ln1_g + ln1_b).astype(dt)

    u2 = (_layer_norm(x1) * (1.0 + scale2) + shift2).astype(dt)
    h = jnp.square(jax.nn.relu(jnp.einsum('btd,df->btf', u2, w_ff1)))
    f = jnp.einsum('btf,fd->btd', h, w_ff2)
    x2 = (_layer_norm(DEEPNORM_ALPHA * x1 + gate2 * f) * ln2_g + ln2_b).astype(dt)
    return x2, k, v


def setup_inputs(seed: int = 0) -> dict:
    key = jax.random.key(seed)
    ks = jax.random.split(key, 24)
    f32 = jnp.float32
    nrm = lambda k, s: jax.random.normal(k, s, f32)
    return {
        "x_prompt": nrm(ks[0], (BATCH, SEQ, D_MODEL)),
        "x_sample": nrm(ks[1], (DEC_BATCH, DEC_SEQ, D_MODEL)),
        "cache_k": nrm(ks[2], (DEC_BATCH, DEPTH, PAST_LEN, N_KV_HEADS, HEAD_DIM)),
        "cache_v": nrm(ks[3], (DEC_BATCH, DEPTH, PAST_LEN, N_KV_HEADS, HEAD_DIM)),
        "c": nrm(ks[4], (DEC_BATCH, D_MODEL)),
        "c_ctx": nrm(ks[5], (D_MODEL,)),
        "w_mod": nrm(ks[6], (DEPTH, D_MODEL, N_MOD * D_MODEL)) * (0.5 * D_MODEL ** -0.5),
        "b_mod": nrm(ks[7], (DEPTH, N_MOD * D_MODEL)) * 0.02,
        "w_in": nrm(ks[8], (DEPTH, D_MODEL, IN_WIDTH)) * D_MODEL ** -0.5,
        "q_gain": 1.0 + 0.1 * nrm(ks[9], (DEPTH, HEAD_DIM)),
        "k_gain": 1.0 + 0.1 * nrm(ks[10], (DEPTH, HEAD_DIM)),
        "w_pool": nrm(ks[11], (DEPTH, N_POOL_GROUPS, POOL_GROUP_WIDTH, POOL_GROUP_WIDTH)) * POOL_GROUP_WIDTH ** -0.5,
        "pool_scale": 1.0 + 0.1 * nrm(ks[12], (DEPTH, POOL_WIDTH)),
        "w_out": nrm(ks[13], (DEPTH, MIX_WIDTH, D_MODEL)) * (MIX_WIDTH ** -0.5 * DEEPNORM_BETA),
        "ln1_g": 1.0 + 0.1 * nrm(ks[14], (DEPTH, D_MODEL)),
        "ln1_b": 0.02 * nrm(ks[15], (DEPTH, D_MODEL)),
        "w_ff1": nrm(ks[16], (DEPTH, D_MODEL, D_FF)) * D_MODEL ** -0.5,
        "w_ff2": nrm(ks[17], (DEPTH, D_FF, D_MODEL)) * (D_FF ** -0.5 * DEEPNORM_BETA),
        "ln2_g": 1.0 + 0.1 * nrm(ks[18], (DEPTH, D_MODEL)),
        "ln2_b": 0.02 * nrm(ks[19], (DEPTH, D_MODEL)),
    }


def reference(x_prompt, x_sample, cache_k, cache_v, c, c_ctx, w_mod, b_mod, w_in, q_gain, k_gain,
              w_pool, pool_scale, w_out, ln1_g, ln1_b, w_ff1, w_ff2, ln2_g, ln2_b):
    rope = _grid_rope(x_sample.shape[1])
    h_ctx = x_prompt
    h_lat = x_sample
    new_k, new_v = [], []
    for l in range(DEPTH):
        mod_ctx = (jax.nn.silu(c_ctx) @ w_mod[l] + b_mod[l]).reshape(1, N_MOD, D_MODEL)
        mod_lat = (jax.nn.silu(c) @ w_mod[l] + b_mod[l]).reshape(c.shape[0], N_MOD, D_MODEL)
        lw = (w_in[l], q_gain[l], k_gain[l], w_pool[l], pool_scale[l], w_out[l],
              ln1_g[l], ln1_b[l], w_ff1[l], w_ff2[l], ln2_g[l], ln2_b[l])
        h_ctx, k_c, v_c = _layer(h_ctx, mod_ctx, *lw)
        new_k.append(k_c)
        new_v.append(v_c)
        h_lat, _, _ = _layer(h_lat, mod_lat, *lw, rope=rope,
                             k_ctx=cache_k[:, l], v_ctx=cache_v[:, l])
    ctx_k = jnp.stack(new_k, axis=1)
    ctx_v = jnp.stack(new_v, axis=1)
    return (h_ctx, h_lat, ctx_k, ctx_v)
```

```python
import functools

import jax
import jax.numpy as jnp
from jax import lax
from jax.experimental import pallas as pl
from jax.experimental.pallas import tpu as pltpu

HEAD_DIM = 128
N_Q_HEADS = 8
N_KV_HEADS = 2
Q_PER_KV = N_Q_HEADS // N_KV_HEADS
ATTN_WIDTH = N_Q_HEADS * HEAD_DIM
KV_WIDTH = N_KV_HEADS * HEAD_DIM
POOL_WINDOWS = (2, 4, 8, 16)
N_POOL_GROUPS = len(POOL_WINDOWS)
GRID_W = 64
ROPE_THETA = 10000.0
ROPE_QUARTER = HEAD_DIM // 4
EPS = 1e-6
N_MOD = 6
DEPTH = 1
DEEPNORM_ALPHA = (2.0 * DEPTH) ** 0.25
HALO = max(POOL_WINDOWS) // 2
SUBLANES = 8
VMEM_LIMIT = 56 * 1024 * 1024

BF16 = jnp.bfloat16
F32 = jnp.float32


def _params(semantics):
    return pltpu.CompilerParams(dimension_semantics=semantics, vmem_limit_bytes=VMEM_LIMIT)


def _resident(shape):
    return pl.BlockSpec(shape, lambda *_: (0,) * len(shape), pipeline_mode=pl.Buffered(1))


def _layer_norm(x):
    mu = jnp.mean(x, axis=-1, keepdims=True)
    xc = x - mu
    var = jnp.mean(xc * xc, axis=-1, keepdims=True)
    return xc * lax.rsqrt(var + EPS)


def _mod_kernel(c_ref, w_ref, b_ref, o_ref):
    c = c_ref[...]
    s = c * (1.0 / (1.0 + jnp.exp(-c)))
    acc = jnp.dot(s.astype(BF16), w_ref[...].astype(BF16), preferred_element_type=F32)
    o_ref[...] = acc + b_ref[...]


def _modulation(cond, w_mod, b_mod, tn=1024):
    rows, d = cond.shape
    n = w_mod.shape[1]
    return pl.pallas_call(
        _mod_kernel,
        grid=(n // tn,),
        in_specs=[
            pl.BlockSpec((rows, d), lambda j: (0, 0)),
            pl.BlockSpec((d, tn), lambda j: (0, j)),
            pl.BlockSpec((1, tn), lambda j: (0, j)),
        ],
        out_specs=pl.BlockSpec((rows, tn), lambda j: (0, j)),
        out_shape=jax.ShapeDtypeStruct((rows, n), F32),
        compiler_params=_params(("arbitrary",)),
        name="modulation",
    )(cond, w_mod, b_mod.reshape(1, n))


def _swap_rope_halves(x):
    lane = lax.broadcasted_iota(jnp.int32, x.shape, 1)
    first = (lane % (2 * ROPE_QUARTER)) < ROPE_QUARTER
    up = pltpu.roll(x, HEAD_DIM - ROPE_QUARTER, 1)
    down = pltpu.roll(x, ROPE_QUARTER, 1)
    return jnp.where(first, up, down)


def _proj_kernel(*refs, rope, q_scale):
    if rope:
        x_ref, mod_ref, w_ref, qg_ref, kg_ref, cos_ref, sin_ref, q_ref, k_ref, v_ref, p_ref = refs
    else:
        x_ref, mod_ref, w_ref, qg_ref, kg_ref, q_ref, k_ref, v_ref, p_ref = refs
    shift = mod_ref[0, 0:1, :]
    scale = mod_ref[0, 1:2, :]
    u = (_layer_norm(x_ref[...]) * (1.0 + scale) + shift).astype(BF16)

    def head(col, gain_ref, post_scale):
        raw = jnp.dot(u, w_ref[:, col:col + HEAD_DIM], preferred_element_type=F32)
        ms = jnp.mean(raw * raw, axis=-1, keepdims=True)
        y = raw * lax.rsqrt(ms + EPS) * gain_ref[...]
        if rope:
            y = y * cos_ref[...] + _swap_rope_halves(y) * sin_ref[...]
        if post_scale != 1.0:
            y = y * post_scale
        return y

    for h in range(N_Q_HEADS):
        q_ref[:, h * HEAD_DIM:(h + 1) * HEAD_DIM] = head(h * HEAD_DIM, qg_ref, q_scale).astype(q_ref.dtype)
    for h in range(N_KV_HEADS):
        k_ref[:, h * HEAD_DIM:(h + 1) * HEAD_DIM] = head(ATTN_WIDTH + h * HEAD_DIM, kg_ref, 1.0).astype(k_ref.dtype)
    v0 = ATTN_WIDTH + KV_WIDTH
    v_ref[...] = jnp.dot(u, w_ref[:, v0:v0 + KV_WIDTH], preferred_element_type=F32).astype(v_ref.dtype)
    p0 = v0 + KV_WIDTH
    p_ref[...] = jnp.dot(u, w_ref[:, p0:], preferred_element_type=F32).astype(p_ref.dtype)


def _projection(x2d, mod, mod_row, w_in, q_gain, k_gain, rope_tables, seq_len, tm, kv_dtype):
    n_tok, d = x2d.shape
    in_width = w_in.shape[1]
    pool_width = in_width - ATTN_WIDTH - 2 * KV_WIDTH
    rope = rope_tables is not None
    seq_blocks = seq_len // tm if rope else None
    in_specs = [
        pl.BlockSpec((tm, d), lambda i: (i, 0)),
        pl.BlockSpec((1, N_MOD, d), lambda i: (mod_row(i, tm), 0, 0)),
        _resident((d, in_width)),
        _resident((1, HEAD_DIM)),
        _resident((1, HEAD_DIM)),
    ]
    args = [x2d, mod, w_in, q_gain, k_gain]
    if rope:
        in_specs += [pl.BlockSpec((tm, HEAD_DIM), lambda i: (i % seq_blocks, 0))] * 2
        args += list(rope_tables)
    return pl.pallas_call(
        functools.partial(_proj_kernel, rope=rope, q_scale=HEAD_DIM ** -0.5),
        grid=(n_tok // tm,),
        in_specs=in_specs,
        out_specs=[
            pl.BlockSpec((tm, ATTN_WIDTH), lambda i: (i, 0)),
            pl.BlockSpec((tm, KV_WIDTH), lambda i: (i, 0)),
            pl.BlockSpec((tm, KV_WIDTH), lambda i: (i, 0)),
            pl.BlockSpec((tm, pool_width), lambda i: (i, 0)),
        ],
        out_shape=[
            jax.ShapeDtypeStruct((n_tok, ATTN_WIDTH), BF16),
            jax.ShapeDtypeStruct((n_tok, KV_WIDTH), kv_dtype),
            jax.ShapeDtypeStruct((n_tok, KV_WIDTH), kv_dtype),
            jax.ShapeDtypeStruct((n_tok, pool_width), F32),
        ],
        compiler_params=_params(("arbitrary",)),
        name="projection_rope" if rope else "projection",
    )(*args)


def _rope_tables(n_tokens):
    n_rows = n_tokens // GRID_W
    row = jnp.repeat(jnp.arange(n_rows, dtype=F32), GRID_W)
    col = jnp.tile(jnp.arange(GRID_W, dtype=F32), n_rows)
    inv_freq = ROPE_THETA ** (-(jnp.arange(ROPE_QUARTER, dtype=F32) / ROPE_QUARTER))
    ang_r = row[:, None] * inv_freq
    ang_c = col[:, None] * inv_freq
    cos = jnp.concatenate([jnp.cos(ang_r), jnp.cos(ang_r), jnp.cos(ang_c), jnp.cos(ang_c)], axis=-1)
    sin = jnp.concatenate([-jnp.sin(ang_r), jnp.sin(ang_r), -jnp.sin(ang_c), jnp.sin(ang_c)], axis=-1)
    return cos, sin


_NT = (((1,), (1,)), ((), ()))


def _attn_kernel(*refs, n_seg):
    q_ref = refs[0]
    k_refs = refs[1:1 + 2 * n_seg:2]
    v_refs = refs[2:2 + 2 * n_seg:2]
    o_ref = refs[1 + 2 * n_seg]
    for g in range(Q_PER_KV):
        lanes = slice(g * HEAD_DIM, (g + 1) * HEAD_DIM)
        q = q_ref[0, :, lanes]
        scores = [lax.dot_general(q, k_ref[0].astype(BF16), _NT, preferred_element_type=F32)
                  for k_ref in k_refs]
        m = functools.reduce(jnp.maximum, [jnp.max(s, axis=-1, keepdims=True) for s in scores])
        probs = [jnp.exp(s - m) for s in scores]
        denom = functools.reduce(jnp.add, [jnp.sum(p, axis=-1, keepdims=True) for p in probs])
        out = functools.reduce(jnp.add, [
            jnp.dot(p.astype(BF16), v_ref[0].astype(BF16), preferred_element_type=F32)
            for p, v_ref in zip(probs, v_refs)])
        o_ref[0, :, lanes] = (out / denom).astype(o_ref.dtype)


def _attention(q, segments, tq):
    b, t, _ = q.shape
    group_width = Q_PER_KV * HEAD_DIM
    in_specs = [pl.BlockSpec((1, tq, group_width), lambda bi, h, qi: (bi, qi, h))]
    args = [q]
    for k, v in segments:
        s = k.shape[1]
        in_specs += [pl.BlockSpec((1, s, HEAD_DIM), lambda bi, h, qi: (bi, 0, h))] * 2
        args += [k, v]
    return pl.pallas_call(
        functools.partial(_attn_kernel, n_seg=len(segments)),
        grid=(b, N_KV_HEADS, t // tq),
        in_specs=in_specs,
        out_specs=pl.BlockSpec((1, tq, group_width), lambda bi, h, qi: (bi, qi, h)),
        out_shape=jax.ShapeDtypeStruct((b, t, ATTN_WIDTH), BF16),
        compiler_params=_params(("arbitrary", "arbitrary", "arbitrary")),
        name="attention",
    )(*args)


def _window_sum(z, w):
    n = z.shape[0]
    span = 1
    while 2 * span < w:
        z = z + pltpu.roll(z, n - span, 0)
        span *= 2
    return z + pltpu.roll(z, w // 2, 0)


def _mix_kernel(attn_ref, pprev_ref, pcur_ref, pnext_ref, x_ref, mod_ref, wpool_ref, pscale_ref,
                wout_ref, g1_ref, b1_ref, x1_ref, u2_ref, *, tm, seq_len):
    seq_blocks = seq_len // tm
    j = pl.program_id(0) % seq_blocks
    prev = jnp.where(j > 0, pprev_ref[...], 0.0)
    nxt = jnp.where(j < seq_blocks - 1, pnext_ref[...], 0.0)
    cur = pcur_ref[...]
    ext = jnp.concatenate([prev, cur, nxt], axis=0)
    group_width = cur.shape[1] // N_POOL_GROUPS
    pos = j * tm + lax.broadcasted_iota(jnp.int32, (tm, group_width), 0)
    pieces = [attn_ref[...]]
    for g, w in enumerate(POOL_WINDOWS):
        lanes = slice(g * group_width, (g + 1) * group_width)
        total = _window_sum(ext[:, lanes], w)[HALO:HALO + tm]
        count = jnp.minimum(pos + w // 2, seq_len) - jnp.maximum(pos - w // 2, 0)
        pooled = total / count.astype(F32) - cur[:, lanes]
        mixed = jnp.dot(pooled.astype(BF16), wpool_ref[g], preferred_element_type=F32)
        pieces.append((mixed * pscale_ref[:, lanes]).astype(BF16))
    mix = jnp.dot(jnp.concatenate(pieces, axis=-1), wout_ref[...], preferred_element_type=F32)
    gate1 = mod_ref[0, 2:3, :]
    x1 = _layer_norm(DEEPNORM_ALPHA * x_ref[...] + gate1 * mix) * g1_ref[...] + b1_ref[...]
    x1_ref[...] = x1
    shift2 = mod_ref[0, 3:4, :]
    scale2 = mod_ref[0, 4:5, :]
    u2_ref[...] = (_layer_norm(x1) * (1.0 + scale2) + shift2).astype(BF16)


def _mix(attn2d, p2d, x2d, mod, mod_row, w_pool, pool_scale, w_out, ln_g, ln_b, seq_len, tm):
    n_tok, d = x2d.shape
    pool_width = p2d.shape[1]
    assert seq_len % tm == 0, "a pooling block must stay inside one sequence"
    halo_per_block = tm // HALO
    n_halo_blocks = n_tok // HALO
    return pl.pallas_call(
        functools.partial(_mix_kernel, tm=tm, seq_len=seq_len),
        grid=(n_tok // tm,),
        in_specs=[
            pl.BlockSpec((tm, ATTN_WIDTH), lambda i: (i, 0)),
            pl.BlockSpec((HALO, pool_width), lambda i: (jnp.maximum(i * halo_per_block - 1, 0), 0)),
            pl.BlockSpec((tm, pool_width), lambda i: (i, 0)),
            pl.BlockSpec((HALO, pool_width),
                         lambda i: (jnp.minimum((i + 1) * halo_per_block, n_halo_blocks - 1), 0)),
            pl.BlockSpec((tm, d), lambda i: (i, 0)),
            pl.BlockSpec((1, N_MOD, d), lambda i: (mod_row(i, tm), 0, 0)),
            _resident(w_pool.shape),
            _resident((1, pool_width)),
            _resident(w_out.shape),
            _resident((1, d)),
            _resident((1, d)),
        ],
        out_specs=[
            pl.BlockSpec((tm, d), lambda i: (i, 0)),
            pl.BlockSpec((tm, d), lambda i: (i, 0)),
        ],
        out_shape=[
            jax.ShapeDtypeStruct((n_tok, d), F32),
            jax.ShapeDtypeStruct((n_tok, d), BF16),
        ],
        compiler_params=_params(("arbitrary",)),
        name="mix",
    )(attn2d, p2d, p2d, p2d, x2d, mod, w_pool, pool_scale, w_out, ln_g, ln_b)


def _mlp_kernel(u_ref, x1_ref, mod_ref, w1_ref, w2_ref, g2_ref, b2_ref, o_ref, acc_ref):
    j = pl.program_id(1)
    h = jnp.maximum(jnp.dot(u_ref[...], w1_ref[...], preferred_element_type=F32), 0.0)
    part = jnp.dot((h * h).astype(BF16), w2_ref[...], preferred_element_type=F32)

    @pl.when(j == 0)
    def _():
        acc_ref[...] = part

    @pl.when(j > 0)
    def _():
        acc_ref[...] += part

    @pl.when(j == pl.num_programs(1) - 1)
    def _():
        gate2 = mod_ref[0, 5:6, :]
        y = DEEPNORM_ALPHA * x1_ref[...] + gate2 * acc_ref[...]
        o_ref[...] = _layer_norm(y) * g2_ref[...] + b2_ref[...]


def _mlp(u2, x1, mod, mod_row, w1, w2, ln_g, ln_b, tm, tf):
    n_tok, d = x1.shape
    d_ff = w1.shape[1]
    return pl.pallas_call(
        _mlp_kernel,
        grid=(n_tok // tm, d_ff // tf),
        in_specs=[
            pl.BlockSpec((tm, d), lambda i, j: (i, 0)),
            pl.BlockSpec((tm, d), lambda i, j: (i, 0)),
            pl.BlockSpec((1, N_MOD, d), lambda i, j: (mod_row(i, tm), 0, 0)),
            pl.BlockSpec((d, tf), lambda i, j: (0, j)),
            pl.BlockSpec((tf, d), lambda i, j: (j, 0)),
            _resident((1, d)),
            _resident((1, d)),
        ],
        out_specs=pl.BlockSpec((tm, d), lambda i, j: (i, 0)),
        out_shape=jax.ShapeDtypeStruct((n_tok, d), F32),
        scratch_shapes=[pltpu.VMEM((tm, d), F32)],
        compiler_params=_params(("arbitrary", "arbitrary")),
        name="mlp",
    )(u2, x1, mod, w1, w2, ln_g, ln_b)


def _layer(x, mod, mod_row, weights, rope_tables, cache, tm_proj, tq, tm_mix, tm_mlp, tf):
    (w_in, q_gain, k_gain, w_pool, pool_scale, w_out, ln1_g, ln1_b, w_ff1, w_ff2, ln2_g, ln2_b) = weights
    b, t, d = x.shape
    x2d = x.reshape(b * t, d)
    kv_dtype = F32 if cache is None else BF16
    q, k, v, p = _projection(x2d, mod, mod_row, w_in, q_gain, k_gain, rope_tables, t, tm_proj, kv_dtype)
    k3 = k.reshape(b, t, KV_WIDTH)
    v3 = v.reshape(b, t, KV_WIDTH)
    segments = ([] if cache is None else [cache]) + [(k3, v3)]
    attn = _attention(q.reshape(b, t, ATTN_WIDTH), segments, tq)
    x1, u2 = _mix(attn.reshape(b * t, ATTN_WIDTH), p, x2d, mod, mod_row, w_pool, pool_scale, w_out,
                  ln1_g, ln1_b, t, tm_mix)
    y = _mlp(u2, x1, mod, mod_row, w_ff1, w_ff2, ln2_g, ln2_b, tm_mlp, tf)
    return y.reshape(b, t, d), k3, v3


def kernel(x_prompt, x_sample, cache_k, cache_v, c, c_ctx, w_mod, b_mod, w_in, q_gain, k_gain, w_pool,
           pool_scale, w_out, ln1_g, ln1_b, w_ff1, w_ff2, ln2_g, ln2_b):
    assert w_mod.shape[0] == DEPTH
    n_ctx, t_ctx, d = x_prompt.shape
    n_lat, t_lat, _ = x_sample.shape
    past = cache_k.shape[2]

    n_rows = -(-(1 + n_lat) // (2 * SUBLANES)) * (2 * SUBLANES)
    cond = jnp.zeros((n_rows, d), F32).at[0].set(c_ctx).at[1:1 + n_lat].set(c)
    mod = _modulation(cond, w_mod[0], b_mod[0]).reshape(n_rows, N_MOD, d)

    row2 = lambda a: a[0].reshape(1, -1)
    weights = (w_in[0].astype(BF16), row2(q_gain), row2(k_gain), w_pool[0].astype(BF16), row2(pool_scale),
               w_out[0].astype(BF16), row2(ln1_g), row2(ln1_b), w_ff1[0].astype(BF16), w_ff2[0].astype(BF16),
               row2(ln2_g), row2(ln2_b))

    y_ctx, k_ctx, v_ctx = _layer(
        x_prompt, mod, lambda i, tm: 0, weights, None, None,
        tm_proj=512, tq=t_ctx, tm_mix=t_ctx, tm_mlp=512, tf=1024)

    cache = (cache_k[:, 0].reshape(n_lat, past, KV_WIDTH), cache_v[:, 0].reshape(n_lat, past, KV_WIDTH))
    y_lat, _, _ = _layer(
        x_sample, mod, lambda i, tm: 1 + i // (t_lat // tm), weights, _rope_tables(t_lat), cache,
        tm_proj=512, tq=256, tm_mix=512, tm_mlp=512, tf=1024)

    ctx_k = k_ctx.reshape(n_ctx, 1, t_ctx, N_KV_HEADS, HEAD_DIM)
    ctx_v = v_ctx.reshape(n_ctx, 1, t_ctx, N_KV_HEADS, HEAD_DIM)
    return (y_ctx, y_lat, ctx_k, ctx_v)
```

```python
import functools

import jax
import jax.numpy as jnp
from jax import lax
from jax.experimental import pallas as pl
from jax.experimental.pallas import tpu as pltpu

HEAD_DIM = 128
N_Q_HEADS = 8
N_KV_HEADS = 2
Q_PER_KV = N_Q_HEADS // N_KV_HEADS
ATTN_WIDTH = N_Q_HEADS * HEAD_DIM
KV_WIDTH = N_KV_HEADS * HEAD_DIM
POOL_WINDOWS = (2, 4, 8, 16)
N_POOL_GROUPS = len(POOL_WINDOWS)
GRID_W = 64
ROPE_THETA = 10000.0
ROPE_QUARTER = HEAD_DIM // 4
EPS = 1e-6
N_MOD = 6
DEPTH = 1
DEEPNORM_ALPHA = (2.0 * DEPTH) ** 0.25
LOG2_E = 1.4426950408889634
HALO = max(POOL_WINDOWS) // 2
SUBLANES = 8
KEY_CHUNK = 512
VMEM_LIMIT = 56 * 1024 * 1024

BF16 = jnp.bfloat16
F32 = jnp.float32


def _params(semantics):
    return pltpu.CompilerParams(dimension_semantics=semantics, vmem_limit_bytes=VMEM_LIMIT)


def _resident(shape):
    return pl.BlockSpec(shape, lambda *_: (0,) * len(shape), pipeline_mode=pl.Buffered(1))


def _layer_norm(x):
    mu = jnp.mean(x, axis=-1, keepdims=True)
    xc = x - mu
    var = jnp.mean(xc * xc, axis=-1, keepdims=True)
    return xc * lax.rsqrt(var + EPS)


def _mod_kernel(c_ref, w_ref, b_ref, o_ref):
    c = c_ref[...]
    s = c * (1.0 / (1.0 + jnp.exp(-c)))
    acc = jnp.dot(s.astype(BF16), w_ref[...].astype(BF16), preferred_element_type=F32)
    o_ref[...] = acc + b_ref[...]


def _modulation(cond, w_mod, b_mod, tn=1024):
    rows, d = cond.shape
    n = w_mod.shape[1]
    return pl.pallas_call(
        _mod_kernel,
        grid=(n // tn,),
        in_specs=[
            pl.BlockSpec((rows, d), lambda j: (0, 0)),
            pl.BlockSpec((d, tn), lambda j: (0, j)),
            pl.BlockSpec((1, tn), lambda j: (0, j)),
        ],
        out_specs=pl.BlockSpec((rows, tn), lambda j: (0, j)),
        out_shape=jax.ShapeDtypeStruct((rows, n), F32),
        compiler_params=_params(("arbitrary",)),
        name="modulation",
    )(cond, w_mod, b_mod.reshape(1, n))


def _swap_rope_halves(x):
    lane = lax.broadcasted_iota(jnp.int32, x.shape, 1)
    first = (lane % (2 * ROPE_QUARTER)) < ROPE_QUARTER
    up = pltpu.roll(x, HEAD_DIM - ROPE_QUARTER, 1)
    down = pltpu.roll(x, ROPE_QUARTER, 1)
    return jnp.where(first, up, down)


def _proj_kernel(*refs, rope, q_scale):
    if rope:
        x_ref, mod_ref, w_ref, qg_ref, kg_ref, cos_ref, sin_ref, q_ref, k_ref, v_ref, p_ref = refs
    else:
        x_ref, mod_ref, w_ref, qg_ref, kg_ref, q_ref, k_ref, v_ref, p_ref = refs
    shift = mod_ref[0, 0:1, :]
    scale = mod_ref[0, 1:2, :]
    u = (_layer_norm(x_ref[...]) * (1.0 + scale) + shift).astype(BF16)

    def head(col, gain_ref, post_scale):
        raw = jnp.dot(u, w_ref[:, col:col + HEAD_DIM], preferred_element_type=F32)
        ms = jnp.mean(raw * raw, axis=-1, keepdims=True)
        y = raw * lax.rsqrt(ms + EPS) * gain_ref[...]
        if rope:
            y = y * cos_ref[...] + _swap_rope_halves(y) * sin_ref[...]
        if post_scale != 1.0:
            y = y * post_scale
        return y

    for h in range(N_Q_HEADS):
        q_ref[:, h * HEAD_DIM:(h + 1) * HEAD_DIM] = head(h * HEAD_DIM, qg_ref, q_scale).astype(q_ref.dtype)
    for h in range(N_KV_HEADS):
        k_ref[:, h * HEAD_DIM:(h + 1) * HEAD_DIM] = head(ATTN_WIDTH + h * HEAD_DIM, kg_ref, 1.0).astype(k_ref.dtype)
    v0 = ATTN_WIDTH + KV_WIDTH
    v_ref[...] = jnp.dot(u, w_ref[:, v0:v0 + KV_WIDTH], preferred_element_type=F32).astype(v_ref.dtype)
    p0 = v0 + KV_WIDTH
    p_ref[...] = jnp.dot(u, w_ref[:, p0:], preferred_element_type=F32).astype(p_ref.dtype)


def _projection(x2d, mod, mod_row, w_in, q_gain, k_gain, rope_tables, seq_len, tm, kv_dtype):
    n_tok, d = x2d.shape
    in_width = w_in.shape[1]
    pool_width = in_width - ATTN_WIDTH - 2 * KV_WIDTH
    rope = rope_tables is not None
    seq_blocks = seq_len // tm if rope else None
    in_specs = [
        pl.BlockSpec((tm, d), lambda i: (i, 0)),
        pl.BlockSpec((1, N_MOD, d), lambda i: (mod_row(i, tm), 0, 0)),
        _resident((d, in_width)),
        _resident((1, HEAD_DIM)),
        _resident((1, HEAD_DIM)),
    ]
    args = [x2d, mod, w_in, q_gain, k_gain]
    if rope:
        in_specs += [pl.BlockSpec((tm, HEAD_DIM), lambda i: (i % seq_blocks, 0))] * 2
        args += list(rope_tables)
    return pl.pallas_call(
        functools.partial(_proj_kernel, rope=rope, q_scale=LOG2_E * HEAD_DIM ** -0.5),
        grid=(n_tok // tm,),
        in_specs=in_specs,
        out_specs=[
            pl.BlockSpec((tm, ATTN_WIDTH), lambda i: (i, 0)),
            pl.BlockSpec((tm, KV_WIDTH), lambda i: (i, 0)),
            pl.BlockSpec((tm, KV_WIDTH), lambda i: (i, 0)),
            pl.BlockSpec((tm, pool_width), lambda i: (i, 0)),
        ],
        out_shape=[
            jax.ShapeDtypeStruct((n_tok, ATTN_WIDTH), BF16),
            jax.ShapeDtypeStruct((n_tok, KV_WIDTH), kv_dtype),
            jax.ShapeDtypeStruct((n_tok, KV_WIDTH), kv_dtype),
            jax.ShapeDtypeStruct((n_tok, pool_width), F32),
        ],
        compiler_params=_params(("arbitrary",)),
        name="projection_rope" if rope else "projection",
    )(*args)


def _rope_tables(n_tokens):
    n_rows = n_tokens // GRID_W
    row = jnp.repeat(jnp.arange(n_rows, dtype=F32), GRID_W)
    col = jnp.tile(jnp.arange(GRID_W, dtype=F32), n_rows)
    inv_freq = ROPE_THETA ** (-(jnp.arange(ROPE_QUARTER, dtype=F32) / ROPE_QUARTER))
    ang_r = row[:, None] * inv_freq
    ang_c = col[:, None] * inv_freq
    cos = jnp.concatenate([jnp.cos(ang_r), jnp.cos(ang_r), jnp.cos(ang_c), jnp.cos(ang_c)], axis=-1)
    sin = jnp.concatenate([-jnp.sin(ang_r), jnp.sin(ang_r), -jnp.sin(ang_c), jnp.sin(ang_c)], axis=-1)
    return cos, sin


_NT = (((1,), (1,)), ((), ()))


def _attn_kernel(*refs, n_seg):
    q_ref = refs[0]
    k_refs = refs[1:1 + 2 * n_seg:2]
    v_refs = refs[2:2 + 2 * n_seg:2]
    o_ref, k_all, vt_all, s_buf = refs[1 + 2 * n_seg:]
    n_keys, tq = s_buf.shape[1:]
    chunk = min(KEY_CHUNK, n_keys)
    n_chunks = n_keys // chunk

    @pl.when(pl.program_id(2) == 0)
    def _():
        row = 0
        for k_ref, v_ref in zip(k_refs, v_refs):
            s = k_ref.shape[1]
            k_all[row:row + s, :] = k_ref[0].astype(BF16)
            vt_all[:, row:row + s] = v_ref[0].astype(F32).T.astype(BF16)
            row += s

    def lanes(g):
        return slice(g * HEAD_DIM, (g + 1) * HEAD_DIM)

    def fold(x, op):
        return op(x.reshape(chunk // SUBLANES, SUBLANES, tq), axis=0)

    def score_chunk(g, c, m8):
        rows = slice(c * chunk, (c + 1) * chunk)
        s = lax.dot_general(k_all[rows, :], q_ref[0, :, lanes(g)], _NT, preferred_element_type=F32)
        s_buf[g % 2, rows, :] = s
        cm = fold(s, jnp.max)
        return cm if m8 is None else jnp.maximum(m8, cm)

    def value_chunk(g, c, m, l8, acc):
        rows = slice(c * chunk, (c + 1) * chunk)
        p = jnp.exp2(s_buf[g % 2, rows, :] - m)
        part = jnp.dot(vt_all[:, rows], p.astype(BF16), preferred_element_type=F32)
        ps = fold(p, jnp.sum)
        return (ps, part) if l8 is None else (l8 + ps, acc + part)

    m_prev = None
    for g in range(Q_PER_KV + 1):
        m8, l8, acc = None, None, None
        for c in range(n_chunks):
            if g < Q_PER_KV:
                m8 = score_chunk(g, c, m8)
            if g > 0:
                l8, acc = value_chunk(g - 1, c, m_prev, l8, acc)
        if g > 0:
            denom = jnp.sum(l8, axis=0, keepdims=True)
            o_ref[0, :, lanes(g - 1)] = (acc / denom).T.astype(o_ref.dtype)
        if g < Q_PER_KV:
            m_prev = jnp.max(m8, axis=0, keepdims=True)


def _attention(q, segments, tq):
    b, t, _ = q.shape
    group_width = Q_PER_KV * HEAD_DIM
    in_specs = [pl.BlockSpec((1, tq, group_width), lambda bi, h, qi: (bi, qi, h))]
    args = [q]
    n_keys = sum(k.shape[1] for k, _ in segments)
    for k, v in segments:
        s = k.shape[1]
        in_specs += [pl.BlockSpec((1, s, HEAD_DIM), lambda bi, h, qi: (bi, 0, h))] * 2
        args += [k, v]
    return pl.pallas_call(
        functools.partial(_attn_kernel, n_seg=len(segments)),
        grid=(b, N_KV_HEADS, t // tq),
        in_specs=in_specs,
        out_specs=pl.BlockSpec((1, tq, group_width), lambda bi, h, qi: (bi, qi, h)),
        out_shape=jax.ShapeDtypeStruct((b, t, ATTN_WIDTH), BF16),
        scratch_shapes=[pltpu.VMEM((n_keys, HEAD_DIM), BF16), pltpu.VMEM((HEAD_DIM, n_keys), BF16),
                        pltpu.VMEM((2, n_keys, tq), F32)],
        compiler_params=_params(("arbitrary", "arbitrary", "arbitrary")),
        name="attention",
    )(*args)


def _window_sum(z, w):
    n = z.shape[0]
    span = 1
    while 2 * span < w:
        z = z + pltpu.roll(z, n - span, 0)
        span *= 2
    return z + pltpu.roll(z, w // 2, 0)


def _mix_kernel(attn_ref, pprev_ref, pcur_ref, pnext_ref, x_ref, mod_ref, wpool_ref, pscale_ref,
                wout_ref, g1_ref, b1_ref, x1_ref, u2_ref, *, tm, seq_len):
    seq_blocks = seq_len // tm
    j = pl.program_id(0) % seq_blocks
    prev = jnp.where(j > 0, pprev_ref[...], 0.0)
    nxt = jnp.where(j < seq_blocks - 1, pnext_ref[...], 0.0)
    cur = pcur_ref[...]
    ext = jnp.concatenate([prev, cur, nxt], axis=0)
    group_width = cur.shape[1] // N_POOL_GROUPS
    pos = j * tm + lax.broadcasted_iota(jnp.int32, (tm, group_width), 0)
    pieces = [attn_ref[...]]
    for g, w in enumerate(POOL_WINDOWS):
        lanes = slice(g * group_width, (g + 1) * group_width)
        total = _window_sum(ext[:, lanes], w)[HALO:HALO + tm]
        count = jnp.minimum(pos + w // 2, seq_len) - jnp.maximum(pos - w // 2, 0)
        pooled = total / count.astype(F32) - cur[:, lanes]
        mixed = jnp.dot(pooled.astype(BF16), wpool_ref[g], preferred_element_type=F32)
        pieces.append((mixed * pscale_ref[:, lanes]).astype(BF16))
    mix = jnp.dot(jnp.concatenate(pieces, axis=-1), wout_ref[...], preferred_element_type=F32)
    gate1 = mod_ref[0, 2:3, :]
    x1 = _layer_norm(DEEPNORM_ALPHA * x_ref[...] + gate1 * mix) * g1_ref[...] + b1_ref[...]
    x1_ref[...] = x1
    shift2 = mod_ref[0, 3:4, :]
    scale2 = mod_ref[0, 4:5, :]
    u2_ref[...] = (_layer_norm(x1) * (1.0 + scale2) + shift2).astype(BF16)


def _mix(attn2d, p2d, x2d, mod, mod_row, w_pool, pool_scale, w_out, ln_g, ln_b, seq_len, tm):
    n_tok, d = x2d.shape
    pool_width = p2d.shape[1]
    assert seq_len % tm == 0, "a pooling block must stay inside one sequence"
    halo_per_block = tm // HALO
    n_halo_blocks = n_tok // HALO
    return pl.pallas_call(
        functools.partial(_mix_kernel, tm=tm, seq_len=seq_len),
        grid=(n_tok // tm,),
        in_specs=[
            pl.BlockSpec((tm, ATTN_WIDTH), lambda i: (i, 0)),
            pl.BlockSpec((HALO, pool_width), lambda i: (jnp.maximum(i * halo_per_block - 1, 0), 0)),
            pl.BlockSpec((tm, pool_width), lambda i: (i, 0)),
            pl.BlockSpec((HALO, pool_width),
                         lambda i: (jnp.minimum((i + 1) * halo_per_block, n_halo_blocks - 1), 0)),
            pl.BlockSpec((tm, d), lambda i: (i, 0)),
            pl.BlockSpec((1, N_MOD, d), lambda i: (mod_row(i, tm), 0, 0)),
            _resident(w_pool.shape),
            _resident((1, pool_width)),
            _resident(w_out.shape),
            _resident((1, d)),
            _resident((1, d)),
        ],
        out_specs=[
            pl.BlockSpec((tm, d), lambda i: (i, 0)),
            pl.BlockSpec((tm, d), lambda i: (i, 0)),
        ],
        out_shape=[
            jax.ShapeDtypeStruct((n_tok, d), F32),
            jax.ShapeDtypeStruct((n_tok, d), BF16),
        ],
        compiler_params=_params(("arbitrary",)),
        name="mix",
    )(attn2d, p2d, p2d, p2d, x2d, mod, w_pool, pool_scale, w_out, ln_g, ln_b)


def _mlp_kernel(u_ref, x1_ref, mod_ref, w1_ref, w2_ref, g2_ref, b2_ref, o_ref, acc_ref):
    j = pl.program_id(1)
    h = jnp.maximum(jnp.dot(u_ref[...], w1_ref[...], preferred_element_type=F32), 0.0)
    part = jnp.dot((h * h).astype(BF16), w2_ref[...], preferred_element_type=F32)

    @pl.when(j == 0)
    def _():
        acc_ref[...] = part

    @pl.when(j > 0)
    def _():
        acc_ref[...] += part

    @pl.when(j == pl.num_programs(1) - 1)
    def _():
        gate2 = mod_ref[0, 5:6, :]
        y = DEEPNORM_ALPHA * x1_ref[...] + gate2 * acc_ref[...]
        o_ref[...] = _layer_norm(y) * g2_ref[...] + b2_ref[...]


def _mlp(u2, x1, mod, mod_row, w1, w2, ln_g, ln_b, tm, tf):
    n_tok, d = x1.shape
    d_ff = w1.shape[1]
    return pl.pallas_call(
        _mlp_kernel,
        grid=(n_tok // tm, d_ff // tf),
        in_specs=[
            pl.BlockSpec((tm, d), lambda i, j: (i, 0)),
            pl.BlockSpec((tm, d), lambda i, j: (i, 0)),
            pl.BlockSpec((1, N_MOD, d), lambda i, j: (mod_row(i, tm), 0, 0)),
            pl.BlockSpec((d, tf), lambda i, j: (0, j)),
            pl.BlockSpec((tf, d), lambda i, j: (j, 0)),
            _resident((1, d)),
            _resident((1, d)),
        ],
        out_specs=pl.BlockSpec((tm, d), lambda i, j: (i, 0)),
        out_shape=jax.ShapeDtypeStruct((n_tok, d), F32),
        scratch_shapes=[pltpu.VMEM((tm, d), F32)],
        compiler_params=_params(("arbitrary", "arbitrary")),
        name="mlp",
    )(u2, x1, mod, w1, w2, ln_g, ln_b)


def _layer(x, mod, mod_row, weights, rope_tables, cache, tm_proj, tq, tm_mix, tm_mlp, tf):
    (w_in, q_gain, k_gain, w_pool, pool_scale, w_out, ln1_g, ln1_b, w_ff1, w_ff2, ln2_g, ln2_b) = weights
    b, t, d = x.shape
    x2d = x.reshape(b * t, d)
    kv_dtype = F32 if cache is None else BF16
    q, k, v, p = _projection(x2d, mod, mod_row, w_in, q_gain, k_gain, rope_tables, t, tm_proj, kv_dtype)
    k3 = k.reshape(b, t, KV_WIDTH)
    v3 = v.reshape(b, t, KV_WIDTH)
    segments = ([] if cache is None else [cache]) + [(k3, v3)]
    attn = _attention(q.reshape(b, t, ATTN_WIDTH), segments, tq)
    x1, u2 = _mix(attn.reshape(b * t, ATTN_WIDTH), p, x2d, mod, mod_row, w_pool, pool_scale, w_out,
                  ln1_g, ln1_b, t, tm_mix)
    y = _mlp(u2, x1, mod, mod_row, w_ff1, w_ff2, ln2_g, ln2_b, tm_mlp, tf)
    return y.reshape(b, t, d), k3, v3


def kernel(x_prompt, x_sample, cache_k, cache_v, c, c_ctx, w_mod, b_mod, w_in, q_gain, k_gain, w_pool,
           pool_scale, w_out, ln1_g, ln1_b, w_ff1, w_ff2, ln2_g, ln2_b):
    assert w_mod.shape[0] == DEPTH
    n_ctx, t_ctx, d = x_prompt.shape
    n_lat, t_lat, _ = x_sample.shape
    past = cache_k.shape[2]

    n_rows = -(-(1 + n_lat) // (2 * SUBLANES)) * (2 * SUBLANES)
    cond = jnp.zeros((n_rows, d), F32).at[0].set(c_ctx).at[1:1 + n_lat].set(c)
    mod = _modulation(cond, w_mod[0], b_mod[0]).reshape(n_rows, N_MOD, d)

    row2 = lambda a: a[0].reshape(1, -1)
    weights = (w_in[0].astype(BF16), row2(q_gain), row2(k_gain), w_pool[0].astype(BF16), row2(pool_scale),
               w_out[0].astype(BF16), row2(ln1_g), row2(ln1_b), w_ff1[0].astype(BF16), w_ff2[0].astype(BF16),
               row2(ln2_g), row2(ln2_b))

    y_ctx, k_ctx, v_ctx = _layer(
        x_prompt, mod, lambda i, tm: 0, weights, None, None,
        tm_proj=512, tq=t_ctx, tm_mix=t_ctx, tm_mlp=512, tf=1024)

    cache = (cache_k[:, 0].reshape(n_lat, past, KV_WIDTH), cache_v[:, 0].reshape(n_lat, past, KV_WIDTH))
    y_lat, _, _ = _layer(
        x_sample, mod, lambda i, tm: 1 + i // (t_lat // tm), weights, _rope_tables(t_lat), cache,
        tm_proj=512, tq=512, tm_mix=512, tm_mlp=512, tf=1024)

    ctx_k = k_ctx.reshape(n_ctx, 1, t_ctx, N_KV_HEADS, HEAD_DIM)
    ctx_v = v_ctx.reshape(n_ctx, 1, t_ctx, N_KV_HEADS, HEAD_DIM)
    return (y_ctx, y_lat, ctx_k, ctx_v)
```

```python
import functools

import jax
import jax.numpy as jnp
from jax import lax
from jax.experimental import pallas as pl
from jax.experimental.pallas import tpu as pltpu

HEAD_DIM = 128
N_Q_HEADS = 8
N_KV_HEADS = 2
Q_PER_KV = N_Q_HEADS // N_KV_HEADS
ATTN_WIDTH = N_Q_HEADS * HEAD_DIM
KV_WIDTH = N_KV_HEADS * HEAD_DIM
POOL_WINDOWS = (2, 4, 8, 16)
N_POOL_GROUPS = len(POOL_WINDOWS)
GRID_W = 64
ROPE_THETA = 10000.0
ROPE_QUARTER = HEAD_DIM // 4
EPS = 1e-6
N_MOD = 6
DEPTH = 1
DEEPNORM_ALPHA = (2.0 * DEPTH) ** 0.25
LOG2_E = 1.4426950408889634
HALO = max(POOL_WINDOWS) // 2
SUBLANES = 8
KEY_CHUNK = 1536
VMEM_LIMIT = 56 * 1024 * 1024

BF16 = jnp.bfloat16
F32 = jnp.float32


def _params(semantics):
    return pltpu.CompilerParams(dimension_semantics=semantics, vmem_limit_bytes=VMEM_LIMIT)


def _resident(shape):
    return pl.BlockSpec(shape, lambda *_: (0,) * len(shape), pipeline_mode=pl.Buffered(1))


def _layer_norm(x):
    mu = jnp.mean(x, axis=-1, keepdims=True)
    xc = x - mu
    var = jnp.mean(xc * xc, axis=-1, keepdims=True)
    return xc * lax.rsqrt(var + EPS)


def _mod_kernel(c_ref, w_ref, b_ref, o_ref):
    c = c_ref[...]
    s = c * (1.0 / (1.0 + jnp.exp(-c)))
    acc = jnp.dot(s.astype(BF16), w_ref[...].astype(BF16), preferred_element_type=F32)
    o_ref[...] = acc + b_ref[...]


def _modulation(cond, w_mod, b_mod, tn=1024):
    rows, d = cond.shape
    n = w_mod.shape[1]
    return pl.pallas_call(
        _mod_kernel,
        grid=(n // tn,),
        in_specs=[
            pl.BlockSpec((rows, d), lambda j: (0, 0)),
            pl.BlockSpec((d, tn), lambda j: (0, j)),
            pl.BlockSpec((1, tn), lambda j: (0, j)),
        ],
        out_specs=pl.BlockSpec((rows, tn), lambda j: (0, j)),
        out_shape=jax.ShapeDtypeStruct((rows, n), F32),
        compiler_params=_params(("arbitrary",)),
        name="modulation",
    )(cond, w_mod, b_mod.reshape(1, n))


def _swap_rope_halves(x):
    lane = lax.broadcasted_iota(jnp.int32, x.shape, 1)
    first = (lane % (2 * ROPE_QUARTER)) < ROPE_QUARTER
    up = pltpu.roll(x, HEAD_DIM - ROPE_QUARTER, 1)
    down = pltpu.roll(x, ROPE_QUARTER, 1)
    return jnp.where(first, up, down)


def _proj_kernel(*refs, rope, q_scale):
    if rope:
        x_ref, mod_ref, w_ref, qg_ref, kg_ref, cos_ref, sin_ref, q_ref, k_ref, v_ref, p_ref = refs
    else:
        x_ref, mod_ref, w_ref, qg_ref, kg_ref, q_ref, k_ref, v_ref, p_ref = refs
    shift = mod_ref[0, 0:1, :]
    scale = mod_ref[0, 1:2, :]
    u = (_layer_norm(x_ref[...]) * (1.0 + scale) + shift).astype(BF16)

    def head(col, gain_ref, post_scale):
        raw = jnp.dot(u, w_ref[:, col:col + HEAD_DIM], preferred_element_type=F32)
        ms = jnp.mean(raw * raw, axis=-1, keepdims=True)
        y = raw * lax.rsqrt(ms + EPS) * gain_ref[...]
        if rope:
            y = y * cos_ref[...] + _swap_rope_halves(y) * sin_ref[...]
        if post_scale != 1.0:
            y = y * post_scale
        return y

    for h in range(N_Q_HEADS):
        q_ref[:, h * HEAD_DIM:(h + 1) * HEAD_DIM] = head(h * HEAD_DIM, qg_ref, q_scale).astype(q_ref.dtype)
    for h in range(N_KV_HEADS):
        k_ref[:, h * HEAD_DIM:(h + 1) * HEAD_DIM] = head(ATTN_WIDTH + h * HEAD_DIM, kg_ref, 1.0).astype(k_ref.dtype)
    v0 = ATTN_WIDTH + KV_WIDTH
    v_ref[...] = jnp.dot(u, w_ref[:, v0:v0 + KV_WIDTH], preferred_element_type=F32).astype(v_ref.dtype)
    p0 = v0 + KV_WIDTH
    p_ref[...] = jnp.dot(u, w_ref[:, p0:], preferred_element_type=F32).astype(p_ref.dtype)


def _projection(x2d, mod, mod_row, w_in, q_gain, k_gain, rope_tables, seq_len, tm, kv_dtype):
    n_tok, d = x2d.shape
    in_width = w_in.shape[1]
    pool_width = in_width - ATTN_WIDTH - 2 * KV_WIDTH
    rope = rope_tables is not None
    seq_blocks = seq_len // tm if rope else None
    in_specs = [
        pl.BlockSpec((tm, d), lambda i: (i, 0)),
        pl.BlockSpec((1, N_MOD, d), lambda i: (mod_row(i, tm), 0, 0)),
        _resident((d, in_width)),
        _resident((1, HEAD_DIM)),
        _resident((1, HEAD_DIM)),
    ]
    args = [x2d, mod, w_in, q_gain, k_gain]
    if rope:
        in_specs += [pl.BlockSpec((tm, HEAD_DIM), lambda i: (i % seq_blocks, 0))] * 2
        args += list(rope_tables)
    return pl.pallas_call(
        functools.partial(_proj_kernel, rope=rope, q_scale=LOG2_E * HEAD_DIM ** -0.5),
        grid=(n_tok // tm,),
        in_specs=in_specs,
        out_specs=[
            pl.BlockSpec((tm, ATTN_WIDTH), lambda i: (i, 0)),
            pl.BlockSpec((tm, KV_WIDTH), lambda i: (i, 0)),
            pl.BlockSpec((tm, KV_WIDTH), lambda i: (i, 0)),
            pl.BlockSpec((tm, pool_width), lambda i: (i, 0)),
        ],
        out_shape=[
            jax.ShapeDtypeStruct((n_tok, ATTN_WIDTH), BF16),
            jax.ShapeDtypeStruct((n_tok, KV_WIDTH), kv_dtype),
            jax.ShapeDtypeStruct((n_tok, KV_WIDTH), kv_dtype),
            jax.ShapeDtypeStruct((n_tok, pool_width), F32),
        ],
        compiler_params=_params(("arbitrary",)),
        name="projection_rope" if rope else "projection",
    )(*args)


def _rope_tables(n_tokens):
    n_rows = n_tokens // GRID_W
    inv_freq = ROPE_THETA ** (-(jnp.arange(ROPE_QUARTER, dtype=F32) / ROPE_QUARTER))
    ang_r = jnp.arange(n_rows, dtype=F32)[:, None] * inv_freq
    ang_c = jnp.arange(GRID_W, dtype=F32)[:, None] * inv_freq
    by_row = lambda a: jnp.repeat(a, GRID_W, axis=0)
    by_col = lambda a: jnp.tile(a, (n_rows, 1))
    cos_r, sin_r = by_row(jnp.cos(ang_r)), by_row(jnp.sin(ang_r))
    cos_c, sin_c = by_col(jnp.cos(ang_c)), by_col(jnp.sin(ang_c))
    cos = jnp.concatenate([cos_r, cos_r, cos_c, cos_c], axis=-1)
    sin = jnp.concatenate([-sin_r, sin_r, -sin_c, sin_c], axis=-1)
    return cos, sin


_NT = (((1,), (1,)), ((), ()))


def _attn_kernel(*refs, n_seg):
    q_ref = refs[0]
    k_refs = refs[1:1 + 2 * n_seg:2]
    v_refs = refs[2:2 + 2 * n_seg:2]
    o_ref, k_all, vt_all, s_buf = refs[1 + 2 * n_seg:]
    n_keys, tq = s_buf.shape[1:]
    chunk = min(KEY_CHUNK, n_keys)
    n_chunks = n_keys // chunk

    @pl.when(pl.program_id(2) == 0)
    def _():
        row = 0
        for k_ref, v_ref in zip(k_refs, v_refs):
            s = k_ref.shape[1]
            k_all[row:row + s, :] = k_ref[0].astype(BF16)
            vt_all[:, row:row + s] = v_ref[0].astype(F32).T.astype(BF16)
            row += s

    def lanes(g):
        return slice(g * HEAD_DIM, (g + 1) * HEAD_DIM)

    def fold(x, op):
        return op(x.reshape(chunk // SUBLANES, SUBLANES, tq), axis=0)

    def score_chunk(g, c, m8):
        rows = slice(c * chunk, (c + 1) * chunk)
        s = lax.dot_general(k_all[rows, :], q_ref[0, :, lanes(g)], _NT, preferred_element_type=F32)
        s_buf[g % 2, rows, :] = s
        cm = fold(s, jnp.max)
        return cm if m8 is None else jnp.maximum(m8, cm)

    def value_chunk(g, c, m, l8, acc):
        rows = slice(c * chunk, (c + 1) * chunk)
        p = jnp.exp2(s_buf[g % 2, rows, :] - m)
        part = jnp.dot(vt_all[:, rows], p.astype(BF16), preferred_element_type=F32)
        ps = fold(p, jnp.sum)
        return (ps, part) if l8 is None else (l8 + ps, acc + part)

    m_prev = None
    for g in range(Q_PER_KV + 1):
        m8, l8, acc = None, None, None
        for c in range(n_chunks):
            if g < Q_PER_KV:
                m8 = score_chunk(g, c, m8)
            if g > 0:
                l8, acc = value_chunk(g - 1, c, m_prev, l8, acc)
        if g > 0:
            denom = jnp.sum(l8, axis=0, keepdims=True)
            o_ref[0, :, lanes(g - 1)] = (acc / denom).T.astype(o_ref.dtype)
        if g < Q_PER_KV:
            m_prev = jnp.max(m8, axis=0, keepdims=True)


def _attention(q, segments, tq):
    b, t, _ = q.shape
    group_width = Q_PER_KV * HEAD_DIM
    in_specs = [pl.BlockSpec((1, tq, group_width), lambda bi, h, qi: (bi, qi, h))]
    args = [q]
    n_keys = sum(k.shape[1] for k, _ in segments)
    assert n_keys % min(KEY_CHUNK, n_keys) == 0
    for k, v in segments:
        s = k.shape[1]
        in_specs += [pl.BlockSpec((1, s, HEAD_DIM), lambda bi, h, qi: (bi, 0, h))] * 2
        args += [k, v]
    return pl.pallas_call(
        functools.partial(_attn_kernel, n_seg=len(segments)),
        grid=(b, N_KV_HEADS, t // tq),
        in_specs=in_specs,
        out_specs=pl.BlockSpec((1, tq, group_width), lambda bi, h, qi: (bi, qi, h)),
        out_shape=jax.ShapeDtypeStruct((b, t, ATTN_WIDTH), BF16),
        scratch_shapes=[pltpu.VMEM((n_keys, HEAD_DIM), BF16), pltpu.VMEM((HEAD_DIM, n_keys), BF16),
                        pltpu.VMEM((2, n_keys, tq), F32)],
        compiler_params=_params(("arbitrary", "arbitrary", "arbitrary")),
        name="attention",
    )(*args)


def _window_sum(z, w):
    n = z.shape[0]
    span = 1
    while 2 * span < w:
        z = z + pltpu.roll(z, n - span, 0)
        span *= 2
    return z + pltpu.roll(z, w // 2, 0)


def _mix_kernel(attn_ref, pprev_ref, pcur_ref, pnext_ref, x_ref, mod_ref, wpool_ref, pscale_ref,
                wout_ref, g1_ref, b1_ref, x1_ref, u2_ref, *, tm, seq_len):
    seq_blocks = seq_len // tm
    j = pl.program_id(0) % seq_blocks
    prev = jnp.where(j > 0, pprev_ref[...], 0.0)
    nxt = jnp.where(j < seq_blocks - 1, pnext_ref[...], 0.0)
    cur = pcur_ref[...]
    ext = jnp.concatenate([prev, cur, nxt], axis=0)
    group_width = cur.shape[1] // N_POOL_GROUPS
    pos = j * tm + lax.broadcasted_iota(jnp.int32, (tm, group_width), 0)
    pieces = [attn_ref[...]]
    for g, w in enumerate(POOL_WINDOWS):
        lanes = slice(g * group_width, (g + 1) * group_width)
        total = _window_sum(ext[:, lanes], w)[HALO:HALO + tm]
        count = jnp.minimum(pos + w // 2, seq_len) - jnp.maximum(pos - w // 2, 0)
        pooled = total / count.astype(F32) - cur[:, lanes]
        mixed = jnp.dot(pooled.astype(BF16), wpool_ref[g], preferred_element_type=F32)
        pieces.append((mixed * pscale_ref[:, lanes]).astype(BF16))
    mix = jnp.dot(jnp.concatenate(pieces, axis=-1), wout_ref[...], preferred_element_type=F32)
    gate1 = mod_ref[0, 2:3, :]
    x1 = _layer_norm(DEEPNORM_ALPHA * x_ref[...] + gate1 * mix) * g1_ref[...] + b1_ref[...]
    x1_ref[...] = x1
    shift2 = mod_ref[0, 3:4, :]
    scale2 = mod_ref[0, 4:5, :]
    u2_ref[...] = (_layer_norm(x1) * (1.0 + scale2) + shift2).astype(BF16)


def _mix(attn2d, p2d, x2d, mod, mod_row, w_pool, pool_scale, w_out, ln_g, ln_b, seq_len, tm):
    n_tok, d = x2d.shape
    pool_width = p2d.shape[1]
    assert seq_len % tm == 0, "a pooling block must stay inside one sequence"
    halo_per_block = tm // HALO
    n_halo_blocks = n_tok // HALO
    return pl.pallas_call(
        functools.partial(_mix_kernel, tm=tm, seq_len=seq_len),
        grid=(n_tok // tm,),
        in_specs=[
            pl.BlockSpec((tm, ATTN_WIDTH), lambda i: (i, 0)),
            pl.BlockSpec((HALO, pool_width), lambda i: (jnp.maximum(i * halo_per_block - 1, 0), 0)),
            pl.BlockSpec((tm, pool_width), lambda i: (i, 0)),
            pl.BlockSpec((HALO, pool_width),
                         lambda i: (jnp.minimum((i + 1) * halo_per_block, n_halo_blocks - 1), 0)),
            pl.BlockSpec((tm, d), lambda i: (i, 0)),
            pl.BlockSpec((1, N_MOD, d), lambda i: (mod_row(i, tm), 0, 0)),
            _resident(w_pool.shape),
            _resident((1, pool_width)),
            _resident(w_out.shape),
            _resident((1, d)),
            _resident((1, d)),
        ],
        out_specs=[
            pl.BlockSpec((tm, d), lambda i: (i, 0)),
            pl.BlockSpec((tm, d), lambda i: (i, 0)),
        ],
        out_shape=[
            jax.ShapeDtypeStruct((n_tok, d), F32),
            jax.ShapeDtypeStruct((n_tok, d), BF16),
        ],
        compiler_params=_params(("arbitrary",)),
        name="mix",
    )(attn2d, p2d, p2d, p2d, x2d, mod, w_pool, pool_scale, w_out, ln_g, ln_b)


def _mlp_kernel(u_ref, x1_ref, mod_ref, w1_ref, w2_ref, g2_ref, b2_ref, o_ref, acc_ref):
    j = pl.program_id(1)

    @pl.when(j == 0)
    def _():
        acc_ref[...] = jnp.zeros_like(acc_ref)

    h = jnp.maximum(jnp.dot(u_ref[...], w1_ref[...], preferred_element_type=F32), 0.0)
    acc_ref[...] += jnp.dot((h * h).astype(BF16), w2_ref[...], preferred_element_type=F32)

    @pl.when(j == pl.num_programs(1) - 1)
    def _():
        gate2 = mod_ref[0, 5:6, :]
        y = DEEPNORM_ALPHA * x1_ref[...] + gate2 * acc_ref[...]
        o_ref[...] = _layer_norm(y) * g2_ref[...] + b2_ref[...]


def _mlp(u2, x1, mod, mod_row, w1, w2, ln_g, ln_b, tm, tf):
    n_tok, d = x1.shape
    d_ff = w1.shape[1]
    return pl.pallas_call(
        _mlp_kernel,
        grid=(n_tok // tm, d_ff // tf),
        in_specs=[
            pl.BlockSpec((tm, d), lambda i, j: (i, 0)),
            pl.BlockSpec((tm, d), lambda i, j: (i, 0)),
            pl.BlockSpec((1, N_MOD, d), lambda i, j: (mod_row(i, tm), 0, 0)),
            pl.BlockSpec((d, tf), lambda i, j: (0, j)),
            pl.BlockSpec((tf, d), lambda i, j: (j, 0)),
            _resident((1, d)),
            _resident((1, d)),
        ],
        out_specs=pl.BlockSpec((tm, d), lambda i, j: (i, 0)),
        out_shape=jax.ShapeDtypeStruct((n_tok, d), F32),
        scratch_shapes=[pltpu.VMEM((tm, d), F32)],
        compiler_params=_params(("arbitrary", "arbitrary")),
        name="mlp",
    )(u2, x1, mod, w1, w2, ln_g, ln_b)


def _layer(x, mod, mod_row, weights, rope_tables, cache, tm_proj, tq, tm_mix, tm_mlp, tf):
    (w_in, q_gain, k_gain, w_pool, pool_scale, w_out, ln1_g, ln1_b, w_ff1, w_ff2, ln2_g, ln2_b) = weights
    b, t, d = x.shape
    x2d = x.reshape(b * t, d)
    kv_dtype = F32 if cache is None else BF16
    q, k, v, p = _projection(x2d, mod, mod_row, w_in, q_gain, k_gain, rope_tables, t, tm_proj, kv_dtype)
    k3 = k.reshape(b, t, KV_WIDTH)
    v3 = v.reshape(b, t, KV_WIDTH)
    segments = ([] if cache is None else [cache]) + [(k3, v3)]
    attn = _attention(q.reshape(b, t, ATTN_WIDTH), segments, tq)
    x1, u2 = _mix(attn.reshape(b * t, ATTN_WIDTH), p, x2d, mod, mod_row, w_pool, pool_scale, w_out,
                  ln1_g, ln1_b, t, tm_mix)
    y = _mlp(u2, x1, mod, mod_row, w_ff1, w_ff2, ln2_g, ln2_b, tm_mlp, tf)
    return y.reshape(b, t, d), k3, v3


def kernel(x_prompt, x_sample, cache_k, cache_v, c, c_ctx, w_mod, b_mod, w_in, q_gain, k_gain, w_pool,
           pool_scale, w_out, ln1_g, ln1_b, w_ff1, w_ff2, ln2_g, ln2_b):
    assert w_mod.shape[0] == DEPTH
    n_ctx, t_ctx, d = x_prompt.shape
    n_lat, t_lat, _ = x_sample.shape
    past = cache_k.shape[2]

    n_rows = -(-(1 + n_lat) // (2 * SUBLANES)) * (2 * SUBLANES)
    cond = jnp.zeros((n_rows, d), F32).at[0].set(c_ctx).at[1:1 + n_lat].set(c)
    mod = _modulation(cond, w_mod[0], b_mod[0]).reshape(n_rows, N_MOD, d)

    row2 = lambda a: a[0].reshape(1, -1)
    weights = (w_in[0].astype(BF16), row2(q_gain), row2(k_gain), w_pool[0].astype(BF16), row2(pool_scale),
               w_out[0].astype(BF16), row2(ln1_g), row2(ln1_b), w_ff1[0].astype(BF16), w_ff2[0].astype(BF16),
               row2(ln2_g), row2(ln2_b))

    y_ctx, k_ctx, v_ctx = _layer(
        x_prompt, mod, lambda i, tm: 0, weights, None, None,
        tm_proj=512, tq=t_ctx, tm_mix=t_ctx, tm_mlp=512, tf=1024)

    cache = (cache_k[:, 0].reshape(n_lat, past, KV_WIDTH), cache_v[:, 0].reshape(n_lat, past, KV_WIDTH))
    y_lat, _, _ = _layer(
        x_sample, mod, lambda i, tm: 1 + i // (t_lat // tm), weights, _rope_tables(t_lat), cache,
        tm_proj=512, tq=512, tm_mix=512, tm_mlp=512, tf=1024)

    ctx_k = k_ctx.reshape(n_ctx, 1, t_ctx, N_KV_HEADS, HEAD_DIM)
    ctx_v = v_ctx.reshape(n_ctx, 1, t_ctx, N_KV_HEADS, HEAD_DIM)
    return (y_ctx, y_lat, ctx_k, ctx_v)
```

```python
import functools

import jax
import jax.numpy as jnp
from jax import lax
from jax.experimental import pallas as pl
from jax.experimental.pallas import tpu as pltpu

HEAD_DIM = 128
N_Q_HEADS = 8
N_KV_HEADS = 2
Q_PER_KV = N_Q_HEADS // N_KV_HEADS
ATTN_WIDTH = N_Q_HEADS * HEAD_DIM
KV_WIDTH = N_KV_HEADS * HEAD_DIM
POOL_WINDOWS = (2, 4, 8, 16)
N_POOL_GROUPS = len(POOL_WINDOWS)
GRID_W = 64
ROPE_THETA = 10000.0
ROPE_QUARTER = HEAD_DIM // 4
EPS = 1e-6
N_MOD = 6
DEPTH = 1
DEEPNORM_ALPHA = (2.0 * DEPTH) ** 0.25
LOG2_E = 1.4426950408889634
HALO = max(POOL_WINDOWS) // 2
SUBLANES = 8
KEY_CHUNK = 1536
SUB = 256
MXU_COLS = 256
HEADS_PER_DOT = MXU_COLS // HEAD_DIM
VMEM_LIMIT = 56 * 1024 * 1024

BF16 = jnp.bfloat16
F32 = jnp.float32


def _params(semantics):
    return pltpu.CompilerParams(dimension_semantics=semantics, vmem_limit_bytes=VMEM_LIMIT)


def _resident(shape):
    return pl.BlockSpec(shape, lambda *_: (0,) * len(shape), pipeline_mode=pl.Buffered(1))


def _layer_norm(x):
    mu = jnp.mean(x, axis=-1, keepdims=True)
    xc = x - mu
    var = jnp.mean(xc * xc, axis=-1, keepdims=True)
    return xc * lax.rsqrt(var + EPS)


def _mod_kernel(c_ref, w_ref, b_ref, o_ref):
    c = c_ref[...]
    s = c * (1.0 / (1.0 + jnp.exp(-c)))
    acc = jnp.dot(s.astype(BF16), w_ref[...].astype(BF16), preferred_element_type=F32)
    o_ref[...] = acc + b_ref[...]


def _modulation(cond, w_mod, b_mod, tn=1024):
    rows, d = cond.shape
    n = w_mod.shape[1]
    return pl.pallas_call(
        _mod_kernel,
        grid=(n // tn,),
        in_specs=[
            pl.BlockSpec((rows, d), lambda j: (0, 0)),
            pl.BlockSpec((d, tn), lambda j: (0, j)),
            pl.BlockSpec((1, tn), lambda j: (0, j)),
        ],
        out_specs=pl.BlockSpec((rows, tn), lambda j: (0, j)),
        out_shape=jax.ShapeDtypeStruct((rows, n), F32),
        compiler_params=_params(("arbitrary",)),
        name="modulation",
    )(cond, w_mod, b_mod.reshape(1, n))


def _swap_rope_halves(x):
    lane = lax.broadcasted_iota(jnp.int32, x.shape, 1)
    first = (lane % (2 * ROPE_QUARTER)) < ROPE_QUARTER
    up = pltpu.roll(x, HEAD_DIM - ROPE_QUARTER, 1)
    down = pltpu.roll(x, ROPE_QUARTER, 1)
    return jnp.where(first, up, down)


def _proj_kernel(*refs, rope, q_scale):
    if rope:
        x_ref, mod_ref, w_ref, qg_ref, kg_ref, cos_ref, sin_ref, q_ref, k_ref, v_ref, p_ref = refs
    else:
        x_ref, mod_ref, w_ref, qg_ref, kg_ref, q_ref, k_ref, v_ref, p_ref = refs
    shift = mod_ref[0, 0:1, :]
    scale = mod_ref[0, 1:2, :]
    v0 = ATTN_WIDTH + KV_WIDTH
    p0 = v0 + KV_WIDTH

    for half in range(x_ref.shape[0] // SUB):
        body = slice(half * SUB, (half + 1) * SUB)
        u = (_layer_norm(x_ref[body, :]) * (1.0 + scale) + shift).astype(BF16)

        def heads(out_ref, col0, n_heads, gain_ref, post_scale):
            for first in range(0, n_heads, HEADS_PER_DOT):
                c0 = col0 + first * HEAD_DIM
                raw_all = jnp.dot(u, w_ref[:, c0:c0 + MXU_COLS], preferred_element_type=F32)
                for i in range(HEADS_PER_DOT):
                    raw = raw_all[:, i * HEAD_DIM:(i + 1) * HEAD_DIM]
                    ms = jnp.mean(raw * raw, axis=-1, keepdims=True)
                    y = raw * lax.rsqrt(ms + EPS) * gain_ref[...]
                    if rope:
                        y = y * cos_ref[body, :] + _swap_rope_halves(y) * sin_ref[body, :]
                    if post_scale != 1.0:
                        y = y * post_scale
                    out = (first + i) * HEAD_DIM
                    out_ref[body, out:out + HEAD_DIM] = y.astype(out_ref.dtype)

        heads(q_ref, 0, N_Q_HEADS, qg_ref, q_scale)
        heads(k_ref, ATTN_WIDTH, N_KV_HEADS, kg_ref, 1.0)
        v_ref[body, :] = jnp.dot(u, w_ref[:, v0:v0 + KV_WIDTH], preferred_element_type=F32).astype(v_ref.dtype)
        p_ref[body, :] = jnp.dot(u, w_ref[:, p0:], preferred_element_type=F32).astype(p_ref.dtype)


def _projection(x2d, mod, mod_row, w_in, q_gain, k_gain, rope_tables, seq_len, tm, kv_dtype):
    n_tok, d = x2d.shape
    in_width = w_in.shape[1]
    pool_width = in_width - ATTN_WIDTH - 2 * KV_WIDTH
    rope = rope_tables is not None
    assert tm % SUB == 0
    seq_blocks = seq_len // tm if rope else None
    in_specs = [
        pl.BlockSpec((tm, d), lambda i: (i, 0)),
        pl.BlockSpec((1, N_MOD, d), lambda i: (mod_row(i, tm), 0, 0)),
        _resident((d, in_width)),
        _resident((1, HEAD_DIM)),
        _resident((1, HEAD_DIM)),
    ]
    args = [x2d, mod, w_in, q_gain, k_gain]
    if rope:
        in_specs += [pl.BlockSpec((tm, HEAD_DIM), lambda i: (i % seq_blocks, 0))] * 2
        args += list(rope_tables)
    return pl.pallas_call(
        functools.partial(_proj_kernel, rope=rope, q_scale=LOG2_E * HEAD_DIM ** -0.5),
        grid=(n_tok // tm,),
        in_specs=in_specs,
        out_specs=[
            pl.BlockSpec((tm, ATTN_WIDTH), lambda i: (i, 0)),
            pl.BlockSpec((tm, KV_WIDTH), lambda i: (i, 0)),
            pl.BlockSpec((tm, KV_WIDTH), lambda i: (i, 0)),
            pl.BlockSpec((tm, pool_width), lambda i: (i, 0)),
        ],
        out_shape=[
            jax.ShapeDtypeStruct((n_tok, ATTN_WIDTH), BF16),
            jax.ShapeDtypeStruct((n_tok, KV_WIDTH), kv_dtype),
            jax.ShapeDtypeStruct((n_tok, KV_WIDTH), kv_dtype),
            jax.ShapeDtypeStruct((n_tok, pool_width), F32),
        ],
        compiler_params=_params(("arbitrary",)),
        name="projection_rope" if rope else "projection",
    )(*args)


def _rope_tables(n_tokens):
    n_rows = n_tokens // GRID_W
    inv_freq = ROPE_THETA ** (-(jnp.arange(ROPE_QUARTER, dtype=F32) / ROPE_QUARTER))
    ang_r = jnp.arange(n_rows, dtype=F32)[:, None] * inv_freq
    ang_c = jnp.arange(GRID_W, dtype=F32)[:, None] * inv_freq
    by_row = lambda a: jnp.repeat(a, GRID_W, axis=0)
    by_col = lambda a: jnp.tile(a, (n_rows, 1))
    cos_r, sin_r = by_row(jnp.cos(ang_r)), by_row(jnp.sin(ang_r))
    cos_c, sin_c = by_col(jnp.cos(ang_c)), by_col(jnp.sin(ang_c))
    cos = jnp.concatenate([cos_r, cos_r, cos_c, cos_c], axis=-1)
    sin = jnp.concatenate([-sin_r, sin_r, -sin_c, sin_c], axis=-1)
    return cos, sin


_NT = (((1,), (1,)), ((), ()))


def _attn_kernel(*refs, n_seg):
    q_ref = refs[0]
    k_refs = refs[1:1 + 2 * n_seg:2]
    v_refs = refs[2:2 + 2 * n_seg:2]
    o_ref, k_all, vt_all, s_buf = refs[1 + 2 * n_seg:]
    n_keys, tq = s_buf.shape[1:]
    chunk = min(KEY_CHUNK, n_keys)
    n_chunks = n_keys // chunk

    @pl.when(pl.program_id(2) == 0)
    def _():
        row = 0
        for k_ref, v_ref in zip(k_refs, v_refs):
            s = k_ref.shape[1]
            k_all[row:row + s, :] = k_ref[0].astype(BF16)
            vt_all[:, row:row + s] = v_ref[0].astype(F32).T.astype(BF16)
            row += s

    def lanes(g):
        return slice(g * HEAD_DIM, (g + 1) * HEAD_DIM)

    def fold(x, op):
        return op(x.reshape(chunk // SUBLANES, SUBLANES, tq), axis=0)

    def score_chunk(g, c, m8):
        rows = slice(c * chunk, (c + 1) * chunk)
        s = lax.dot_general(k_all[rows, :], q_ref[0, :, lanes(g)], _NT, preferred_element_type=F32)
        s_buf[g % 2, rows, :] = s
        cm = fold(s, jnp.max)
        return cm if m8 is None else jnp.maximum(m8, cm)

    def value_chunk(g, c, m, l8, acc):
        rows = slice(c * chunk, (c + 1) * chunk)
        p = jnp.exp2(s_buf[g % 2, rows, :] - m)
        part = jnp.dot(vt_all[:, rows], p.astype(BF16), preferred_element_type=F32)
        ps = fold(p, jnp.sum)
        return (ps, part) if l8 is None else (l8 + ps, acc + part)

    m_prev = None
    for g in range(Q_PER_KV + 1):
        m8, l8, acc = None, None, None
        for c in range(n_chunks):
            if g < Q_PER_KV:
                m8 = score_chunk(g, c, m8)
            if g > 0:
                l8, acc = value_chunk(g - 1, c, m_prev, l8, acc)
        if g > 0:
            denom = jnp.sum(l8, axis=0, keepdims=True)
            o_ref[0, :, lanes(g - 1)] = (acc / denom).T.astype(o_ref.dtype)
        if g < Q_PER_KV:
            m_prev = jnp.max(m8, axis=0, keepdims=True)


def _attention(q, segments, tq):
    b, t, _ = q.shape
    group_width = Q_PER_KV * HEAD_DIM
    in_specs = [pl.BlockSpec((1, tq, group_width), lambda bi, h, qi: (bi, qi, h))]
    args = [q]
    n_keys = sum(k.shape[1] for k, _ in segments)
    assert n_keys % min(KEY_CHUNK, n_keys) == 0
    for k, v in segments:
        s = k.shape[1]
        in_specs += [pl.BlockSpec((1, s, HEAD_DIM), lambda bi, h, qi: (bi, 0, h))] * 2
        args += [k, v]
    return pl.pallas_call(
        functools.partial(_attn_kernel, n_seg=len(segments)),
        grid=(b, N_KV_HEADS, t // tq),
        in_specs=in_specs,
        out_specs=pl.BlockSpec((1, tq, group_width), lambda bi, h, qi: (bi, qi, h)),
        out_shape=jax.ShapeDtypeStruct((b, t, ATTN_WIDTH), BF16),
        scratch_shapes=[pltpu.VMEM((n_keys, HEAD_DIM), BF16), pltpu.VMEM((HEAD_DIM, n_keys), BF16),
                        pltpu.VMEM((2, n_keys, tq), F32)],
        compiler_params=_params(("arbitrary", "arbitrary", "arbitrary")),
        name="attention",
    )(*args)


def _window_sum(z, w):
    n = z.shape[0]
    span = 1
    while 2 * span < w:
        z = z + pltpu.roll(z, n - span, 0)
        span *= 2
    return z + pltpu.roll(z, w // 2, 0)


def _mix_kernel(attn_ref, pprev_ref, pcur_ref, pnext_ref, x_ref, mod_ref, wpool_ref, pscale_ref,
                wout_ref, g1_ref, b1_ref, x1_ref, u2_ref, *, tm, seq_len):
    group_width = pcur_ref.shape[1] // N_POOL_GROUPS
    block_pos = (pl.program_id(0) * tm) % seq_len
    n_half = tm // SUB
    gate1 = mod_ref[0, 2:3, :]
    shift2 = mod_ref[0, 3:4, :]
    scale2 = mod_ref[0, 4:5, :]

    for half in range(n_half):
        body = slice(half * SUB, (half + 1) * SUB)
        pos0 = (block_pos + half * SUB) % seq_len
        before = pprev_ref[...] if half == 0 else pcur_ref[half * SUB - HALO:half * SUB, :]
        after = pnext_ref[...] if half == n_half - 1 else pcur_ref[(half + 1) * SUB:(half + 1) * SUB + HALO, :]
        before = jnp.where(pos0 > 0, before, 0.0)
        after = jnp.where(pos0 + SUB < seq_len, after, 0.0)
        cur = pcur_ref[body, :]
        ext = jnp.concatenate([before, cur, after], axis=0)
        pos = pos0 + lax.broadcasted_iota(jnp.int32, (SUB, group_width), 0)
        pieces = [attn_ref[body, :]]
        for g, w in enumerate(POOL_WINDOWS):
            lanes = slice(g * group_width, (g + 1) * group_width)
            total = _window_sum(ext[:, lanes], w)[HALO:HALO + SUB]
            count = jnp.minimum(pos + w // 2, seq_len) - jnp.maximum(pos - w // 2, 0)
            pooled = total / count.astype(F32) - cur[:, lanes]
            mixed = jnp.dot(pooled.astype(BF16), wpool_ref[g], preferred_element_type=F32)
            pieces.append((mixed * pscale_ref[:, lanes]).astype(BF16))
        mix = jnp.dot(jnp.concatenate(pieces, axis=-1), wout_ref[...], preferred_element_type=F32)
        x1 = _layer_norm(DEEPNORM_ALPHA * x_ref[body, :] + gate1 * mix) * g1_ref[...] + b1_ref[...]
        x1_ref[body, :] = x1
        u2_ref[body, :] = (_layer_norm(x1) * (1.0 + scale2) + shift2).astype(BF16)


def _mix(attn2d, p2d, x2d, mod, mod_row, w_pool, pool_scale, w_out, ln_g, ln_b, seq_len, tm):
    n_tok, d = x2d.shape
    pool_width = p2d.shape[1]
    assert tm % SUB == 0 and seq_len % SUB == 0 and (seq_len % tm == 0 or tm % seq_len == 0)
    halo_per_block = tm // HALO
    n_halo_blocks = n_tok // HALO
    return pl.pallas_call(
        functools.partial(_mix_kernel, tm=tm, seq_len=seq_len),
        grid=(n_tok // tm,),
        in_specs=[
            pl.BlockSpec((tm, ATTN_WIDTH), lambda i: (i, 0)),
            pl.BlockSpec((HALO, pool_width), lambda i: (jnp.maximum(i * halo_per_block - 1, 0), 0)),
            pl.BlockSpec((tm, pool_width), lambda i: (i, 0)),
            pl.BlockSpec((HALO, pool_width),
                         lambda i: (jnp.minimum((i + 1) * halo_per_block, n_halo_blocks - 1), 0)),
            pl.BlockSpec((tm, d), lambda i: (i, 0)),
            pl.BlockSpec((1, N_MOD, d), lambda i: (mod_row(i, tm), 0, 0)),
            _resident(w_pool.shape),
            _resident((1, pool_width)),
            _resident(w_out.shape),
            _resident((1, d)),
            _resident((1, d)),
        ],
        out_specs=[
            pl.BlockSpec((tm, d), lambda i: (i, 0)),
            pl.BlockSpec((tm, d), lambda i: (i, 0)),
        ],
        out_shape=[
            jax.ShapeDtypeStruct((n_tok, d), F32),
            jax.ShapeDtypeStruct((n_tok, d), BF16),
        ],
        compiler_params=_params(("arbitrary",)),
        name="mix",
    )(attn2d, p2d, p2d, p2d, x2d, mod, w_pool, pool_scale, w_out, ln_g, ln_b)


def _mlp_kernel(u_ref, x1_ref, mod_ref, w1_ref, w2_ref, g2_ref, b2_ref, o_ref, acc_ref):
    j = pl.program_id(1)

    @pl.when(j == 0)
    def _():
        acc_ref[...] = jnp.zeros_like(acc_ref)

    h = jnp.maximum(jnp.dot(u_ref[...], w1_ref[...], preferred_element_type=F32), 0.0)
    acc_ref[...] += jnp.dot((h * h).astype(BF16), w2_ref[...], preferred_element_type=F32)

    @pl.when(j == pl.num_programs(1) - 1)
    def _():
        gate2 = mod_ref[0, 5:6, :]
        y = DEEPNORM_ALPHA * x1_ref[...] + gate2 * acc_ref[...]
        o_ref[...] = _layer_norm(y) * g2_ref[...] + b2_ref[...]


def _mlp(u2, x1, mod, mod_row, w1, w2, ln_g, ln_b, tm, tf):
    n_tok, d = x1.shape
    d_ff = w1.shape[1]
    return pl.pallas_call(
        _mlp_kernel,
        grid=(n_tok // tm, d_ff // tf),
        in_specs=[
            pl.BlockSpec((tm, d), lambda i, j: (i, 0)),
            pl.BlockSpec((tm, d), lambda i, j: (i, 0)),
            pl.BlockSpec((1, N_MOD, d), lambda i, j: (mod_row(i, tm), 0, 0)),
            pl.BlockSpec((d, tf), lambda i, j: (0, j)),
            pl.BlockSpec((tf, d), lambda i, j: (j, 0)),
            _resident((1, d)),
            _resident((1, d)),
        ],
        out_specs=pl.BlockSpec((tm, d), lambda i, j: (i, 0)),
        out_shape=jax.ShapeDtypeStruct((n_tok, d), F32),
        scratch_shapes=[pltpu.VMEM((tm, d), F32)],
        compiler_params=_params(("arbitrary", "arbitrary")),
        name="mlp",
    )(u2, x1, mod, w1, w2, ln_g, ln_b)


def _layer(x, mod, mod_row, weights, rope_tables, cache, tm_proj, tq, tm_mix, tm_mlp, tf):
    (w_in, q_gain, k_gain, w_pool, pool_scale, w_out, ln1_g, ln1_b, w_ff1, w_ff2, ln2_g, ln2_b) = weights
    b, t, d = x.shape
    x2d = x.reshape(b * t, d)
    kv_dtype = F32 if cache is None else BF16
    q, k, v, p = _projection(x2d, mod, mod_row, w_in, q_gain, k_gain, rope_tables, t, tm_proj, kv_dtype)
    k3 = k.reshape(b, t, KV_WIDTH)
    v3 = v.reshape(b, t, KV_WIDTH)
    segments = ([] if cache is None else [cache]) + [(k3, v3)]
    attn = _attention(q.reshape(b, t, ATTN_WIDTH), segments, tq)
    x1, u2 = _mix(attn.reshape(b * t, ATTN_WIDTH), p, x2d, mod, mod_row, w_pool, pool_scale, w_out,
                  ln1_g, ln1_b, t, tm_mix)
    y = _mlp(u2, x1, mod, mod_row, w_ff1, w_ff2, ln2_g, ln2_b, tm_mlp, tf)
    return y.reshape(b, t, d), k3, v3


def kernel(x_prompt, x_sample, cache_k, cache_v, c, c_ctx, w_mod, b_mod, w_in, q_gain, k_gain, w_pool,
           pool_scale, w_out, ln1_g, ln1_b, w_ff1, w_ff2, ln2_g, ln2_b):
    assert w_mod.shape[0] == DEPTH
    n_ctx, t_ctx, d = x_prompt.shape
    n_lat, t_lat, _ = x_sample.shape
    past = cache_k.shape[2]

    n_rows = -(-(1 + n_lat) // (2 * SUBLANES)) * (2 * SUBLANES)
    cond = jnp.zeros((n_rows, d), F32).at[0].set(c_ctx).at[1:1 + n_lat].set(c)
    mod = _modulation(cond, w_mod[0], b_mod[0]).reshape(n_rows, N_MOD, d)

    row2 = lambda a: a[0].reshape(1, -1)
    weights = (w_in[0].astype(BF16), row2(q_gain), row2(k_gain), w_pool[0].astype(BF16), row2(pool_scale),
               w_out[0].astype(BF16), row2(ln1_g), row2(ln1_b), w_ff1[0].astype(BF16), w_ff2[0].astype(BF16),
               row2(ln2_g), row2(ln2_b))

    y_ctx, k_ctx, v_ctx = _layer(
        x_prompt, mod, lambda i, tm: 0, weights, None, None,
        tm_proj=512, tq=t_ctx, tm_mix=512, tm_mlp=512, tf=1024)

    cache = (cache_k[:, 0].reshape(n_lat, past, KV_WIDTH), cache_v[:, 0].reshape(n_lat, past, KV_WIDTH))
    y_lat, _, _ = _layer(
        x_sample, mod, lambda i, tm: 1 + i // (t_lat // tm), weights, _rope_tables(t_lat), cache,
        tm_proj=512, tq=512, tm_mix=512, tm_mlp=512, tf=1024)

    ctx_k = k_ctx.reshape(n_ctx, 1, t_ctx, N_KV_HEADS, HEAD_DIM)
    ctx_v = v_ctx.reshape(n_ctx, 1, t_ctx, N_KV_HEADS, HEAD_DIM)
    return (y_ctx, y_lat, ctx_k, ctx_v)
```

```python
import functools

import jax
import jax.numpy as jnp
from jax import lax
from jax.experimental import pallas as pl
from jax.experimental.pallas import tpu as pltpu

HEAD_DIM = 128
N_Q_HEADS = 8
N_KV_HEADS = 2
Q_PER_KV = N_Q_HEADS // N_KV_HEADS
ATTN_WIDTH = N_Q_HEADS * HEAD_DIM
KV_WIDTH = N_KV_HEADS * HEAD_DIM
POOL_WINDOWS = (2, 4, 8, 16)
N_POOL_GROUPS = len(POOL_WINDOWS)
GRID_W = 64
ROPE_THETA = 10000.0
ROPE_QUARTER = HEAD_DIM // 4
EPS = 1e-6
N_MOD = 6
DEPTH = 1
DEEPNORM_ALPHA = (2.0 * DEPTH) ** 0.25
LOG2_E = 1.4426950408889634
HALO = max(POOL_WINDOWS) // 2
SUBLANES = 8
KEY_CHUNK = 1536
SUB = 256
MXU_COLS = 256
HEADS_PER_DOT = MXU_COLS // HEAD_DIM
VMEM_LIMIT = 56 * 1024 * 1024

BF16 = jnp.bfloat16
F32 = jnp.float32


def _params(semantics):
    return pltpu.CompilerParams(dimension_semantics=semantics, vmem_limit_bytes=VMEM_LIMIT)


def _resident(shape):
    return pl.BlockSpec(shape, lambda *_: (0,) * len(shape), pipeline_mode=pl.Buffered(1))


def _layer_norm(x):
    mu = jnp.mean(x, axis=-1, keepdims=True)
    xc = x - mu
    var = jnp.mean(xc * xc, axis=-1, keepdims=True)
    return xc * lax.rsqrt(var + EPS)


def _mod_kernel(c_ref, w_ref, b_ref, o_ref):
    c = c_ref[...]
    s = c * (1.0 / (1.0 + jnp.exp(-c)))
    acc = jnp.dot(s.astype(BF16), w_ref[...].astype(BF16), preferred_element_type=F32)
    o_ref[...] = acc + b_ref[...]


def _modulation(cond, w_mod, b_mod, tn=1024):
    rows, d = cond.shape
    n = w_mod.shape[1]
    return pl.pallas_call(
        _mod_kernel,
        grid=(n // tn,),
        in_specs=[
            pl.BlockSpec((rows, d), lambda j: (0, 0)),
            pl.BlockSpec((d, tn), lambda j: (0, j)),
            pl.BlockSpec((1, tn), lambda j: (0, j)),
        ],
        out_specs=pl.BlockSpec((rows, tn), lambda j: (0, j)),
        out_shape=jax.ShapeDtypeStruct((rows, n), F32),
        compiler_params=_params(("arbitrary",)),
        name="modulation",
    )(cond, w_mod, b_mod.reshape(1, n))


def _swap_rope_halves(x):
    lane = lax.broadcasted_iota(jnp.int32, x.shape, 1)
    first = (lane % (2 * ROPE_QUARTER)) < ROPE_QUARTER
    up = pltpu.roll(x, HEAD_DIM - ROPE_QUARTER, 1)
    down = pltpu.roll(x, ROPE_QUARTER, 1)
    return jnp.where(first, up, down)


def _proj_kernel(*refs, rope, q_scale):
    if rope:
        x_ref, mod_ref, w_ref, qg_ref, kg_ref, cos_ref, sin_ref, q_ref, k_ref, v_ref, p_ref = refs
    else:
        x_ref, mod_ref, w_ref, qg_ref, kg_ref, q_ref, k_ref, v_ref, p_ref = refs
    shift = mod_ref[0, 0:1, :]
    scale = mod_ref[0, 1:2, :]
    v0 = ATTN_WIDTH + KV_WIDTH
    p0 = v0 + KV_WIDTH

    for half in range(x_ref.shape[0] // SUB):
        body = slice(half * SUB, (half + 1) * SUB)
        u = (_layer_norm(x_ref[body, :]) * (1.0 + scale) + shift).astype(BF16)

        def heads(out_ref, col0, n_heads, gain_ref, post_scale):
            for first in range(0, n_heads, HEADS_PER_DOT):
                c0 = col0 + first * HEAD_DIM
                raw_all = jnp.dot(u, w_ref[:, c0:c0 + MXU_COLS], preferred_element_type=F32)
                for i in range(HEADS_PER_DOT):
                    raw = raw_all[:, i * HEAD_DIM:(i + 1) * HEAD_DIM]
                    ms = jnp.mean(raw * raw, axis=-1, keepdims=True)
                    y = raw * lax.rsqrt(ms + EPS) * gain_ref[...]
                    if rope:
                        y = y * cos_ref[body, :] + _swap_rope_halves(y) * sin_ref[body, :]
                    if post_scale != 1.0:
                        y = y * post_scale
                    out = (first + i) * HEAD_DIM
                    out_ref[body, out:out + HEAD_DIM] = y.astype(out_ref.dtype)

        heads(q_ref, 0, N_Q_HEADS, qg_ref, q_scale)
        heads(k_ref, ATTN_WIDTH, N_KV_HEADS, kg_ref, 1.0)
        v_ref[body, :] = jnp.dot(u, w_ref[:, v0:v0 + KV_WIDTH], preferred_element_type=F32).astype(v_ref.dtype)
        p_ref[body, :] = jnp.dot(u, w_ref[:, p0:], preferred_element_type=F32).astype(p_ref.dtype)


def _projection(x2d, mod, mod_row, w_in, q_gain, k_gain, rope_tables, seq_len, tm, kv_dtype):
    n_tok, d = x2d.shape
    in_width = w_in.shape[1]
    pool_width = in_width - ATTN_WIDTH - 2 * KV_WIDTH
    rope = rope_tables is not None
    assert tm % SUB == 0
    seq_blocks = seq_len // tm if rope else None
    in_specs = [
        pl.BlockSpec((tm, d), lambda i: (i, 0)),
        pl.BlockSpec((1, N_MOD, d), lambda i: (mod_row(i, tm), 0, 0)),
        _resident((d, in_width)),
        _resident((1, HEAD_DIM)),
        _resident((1, HEAD_DIM)),
    ]
    args = [x2d, mod, w_in, q_gain, k_gain]
    if rope:
        in_specs += [pl.BlockSpec((tm, HEAD_DIM), lambda i: (i % seq_blocks, 0))] * 2
        args += list(rope_tables)
    return pl.pallas_call(
        functools.partial(_proj_kernel, rope=rope, q_scale=LOG2_E * HEAD_DIM ** -0.5),
        grid=(n_tok // tm,),
        in_specs=in_specs,
        out_specs=[
            pl.BlockSpec((tm, ATTN_WIDTH), lambda i: (i, 0)),
            pl.BlockSpec((tm, KV_WIDTH), lambda i: (i, 0)),
            pl.BlockSpec((tm, KV_WIDTH), lambda i: (i, 0)),
            pl.BlockSpec((tm, pool_width), lambda i: (i, 0)),
        ],
        out_shape=[
            jax.ShapeDtypeStruct((n_tok, ATTN_WIDTH), BF16),
            jax.ShapeDtypeStruct((n_tok, KV_WIDTH), kv_dtype),
            jax.ShapeDtypeStruct((n_tok, KV_WIDTH), kv_dtype),
            jax.ShapeDtypeStruct((n_tok, pool_width), F32),
        ],
        compiler_params=_params(("arbitrary",)),
        name="projection_rope" if rope else "projection",
    )(*args)


def _rope_tables(n_tokens):
    n_rows = n_tokens // GRID_W
    inv_freq = ROPE_THETA ** (-(jnp.arange(ROPE_QUARTER, dtype=F32) / ROPE_QUARTER))
    ang_r = jnp.arange(n_rows, dtype=F32)[:, None] * inv_freq
    ang_c = jnp.arange(GRID_W, dtype=F32)[:, None] * inv_freq
    by_row = lambda a: jnp.repeat(a, GRID_W, axis=0)
    by_col = lambda a: jnp.tile(a, (n_rows, 1))
    cos_r, sin_r = by_row(jnp.cos(ang_r)), by_row(jnp.sin(ang_r))
    cos_c, sin_c = by_col(jnp.cos(ang_c)), by_col(jnp.sin(ang_c))
    cos = jnp.concatenate([cos_r, cos_r, cos_c, cos_c], axis=-1)
    sin = jnp.concatenate([-sin_r, sin_r, -sin_c, sin_c], axis=-1)
    return cos, sin


_NT = (((1,), (1,)), ((), ()))


def _attn_kernel(*refs, n_seg):
    q_ref = refs[0]
    k_refs = refs[1:1 + 2 * n_seg:2]
    v_refs = refs[2:2 + 2 * n_seg:2]
    o_ref, k_all, vt_all, s_buf = refs[1 + 2 * n_seg:]
    n_keys, tq = s_buf.shape[1:]
    chunk = min(KEY_CHUNK, n_keys)
    n_chunks = n_keys // chunk

    @pl.when(pl.program_id(2) == 0)
    def _():
        row = 0
        for k_ref, v_ref in zip(k_refs, v_refs):
            s = k_ref.shape[1]
            k_all[row:row + s, :] = k_ref[0].astype(BF16)
            vt_all[:, row:row + s] = v_ref[0].astype(F32).T.astype(BF16)
            row += s

    def lanes(g):
        return slice(g * HEAD_DIM, (g + 1) * HEAD_DIM)

    def fold(x, op):
        return op(x.reshape(chunk // SUBLANES, SUBLANES, tq), axis=0)

    def score_chunk(g, c, m8):
        rows = slice(c * chunk, (c + 1) * chunk)
        s = lax.dot_general(k_all[rows, :], q_ref[0, :, lanes(g)], _NT, preferred_element_type=F32)
        s_buf[g % 2, rows, :] = s
        cm = fold(s, jnp.max)
        return cm if m8 is None else jnp.maximum(m8, cm)

    def value_chunk(g, c, m, l8, acc):
        rows = slice(c * chunk, (c + 1) * chunk)
        p = jnp.exp2(s_buf[g % 2, rows, :] - m)
        part = jnp.dot(vt_all[:, rows], p.astype(BF16), preferred_element_type=F32)
        ps = fold(p, jnp.sum)
        return (ps, part) if l8 is None else (l8 + ps, acc + part)

    m_prev = None
    for g in range(Q_PER_KV + 1):
        m8, l8, acc = None, None, None
        for c in range(n_chunks):
            if g < Q_PER_KV:
                m8 = score_chunk(g, c, m8)
            if g > 0:
                l8, acc = value_chunk(g - 1, c, m_prev, l8, acc)
        if g > 0:
            denom = jnp.sum(l8, axis=0, keepdims=True)
            o_ref[0, :, lanes(g - 1)] = (acc / denom).T.astype(o_ref.dtype)
        if g < Q_PER_KV:
            m_prev = jnp.max(m8, axis=0, keepdims=True)


def _attention(q, segments, tq):
    b, t, _ = q.shape
    group_width = Q_PER_KV * HEAD_DIM
    in_specs = [pl.BlockSpec((1, tq, group_width), lambda bi, h, qi: (bi, qi, h))]
    args = [q]
    n_keys = sum(k.shape[1] for k, _ in segments)
    assert n_keys % min(KEY_CHUNK, n_keys) == 0
    for k, v in segments:
        s = k.shape[1]
        in_specs += [pl.BlockSpec((1, s, HEAD_DIM), lambda bi, h, qi: (bi, 0, h))] * 2
        args += [k, v]
    return pl.pallas_call(
        functools.partial(_attn_kernel, n_seg=len(segments)),
        grid=(b, N_KV_HEADS, t // tq),
        in_specs=in_specs,
        out_specs=pl.BlockSpec((1, tq, group_width), lambda bi, h, qi: (bi, qi, h)),
        out_shape=jax.ShapeDtypeStruct((b, t, ATTN_WIDTH), BF16),
        scratch_shapes=[pltpu.VMEM((n_keys, HEAD_DIM), BF16), pltpu.VMEM((HEAD_DIM, n_keys), BF16),
                        pltpu.VMEM((2, n_keys, tq), F32)],
        compiler_params=_params(("arbitrary", "arbitrary", "arbitrary")),
        name="attention",
    )(*args)


def _window_sum(z, w):
    n = z.shape[0]
    span = 1
    while 2 * span < w:
        z = z + pltpu.roll(z, n - span, 0)
        span *= 2
    return z + pltpu.roll(z, w // 2, 0)


def _mix_kernel(attn_ref, pprev_ref, pcur_ref, pnext_ref, x_ref, mod_ref, wpool_ref, pscale_ref,
                wout_ref, g1_ref, b1_ref, x1_ref, u2_ref, *, tm, seq_len):
    group_width = pcur_ref.shape[1] // N_POOL_GROUPS
    block_pos = (pl.program_id(0) * tm) % seq_len
    n_half = tm // SUB
    gate1 = mod_ref[0, 2:3, :]
    shift2 = mod_ref[0, 3:4, :]
    scale2 = mod_ref[0, 4:5, :]

    for half in range(n_half):
        body = slice(half * SUB, (half + 1) * SUB)
        pos0 = (block_pos + half * SUB) % seq_len
        before = pprev_ref[...] if half == 0 else pcur_ref[half * SUB - HALO:half * SUB, :]
        after = pnext_ref[...] if half == n_half - 1 else pcur_ref[(half + 1) * SUB:(half + 1) * SUB + HALO, :]
        before = jnp.where(pos0 > 0, before, 0.0)
        after = jnp.where(pos0 + SUB < seq_len, after, 0.0)
        cur = pcur_ref[body, :]
        ext = jnp.concatenate([before, cur, after], axis=0)
        pos = pos0 + lax.broadcasted_iota(jnp.int32, (SUB, group_width), 0)
        pieces = [attn_ref[body, :]]
        for g, w in enumerate(POOL_WINDOWS):
            lanes = slice(g * group_width, (g + 1) * group_width)
            total = _window_sum(ext[:, lanes], w)[HALO:HALO + SUB]
            count = jnp.minimum(pos + w // 2, seq_len) - jnp.maximum(pos - w // 2, 0)
            pooled = total / count.astype(F32) - cur[:, lanes]
            mixed = jnp.dot(pooled.astype(BF16), wpool_ref[g], preferred_element_type=F32)
            pieces.append((mixed * pscale_ref[:, lanes]).astype(BF16))
        mix = jnp.dot(jnp.concatenate(pieces, axis=-1), wout_ref[...], preferred_element_type=F32)
        x1 = _layer_norm(DEEPNORM_ALPHA * x_ref[body, :] + gate1 * mix) * g1_ref[...] + b1_ref[...]
        x1_ref[body, :] = x1
        u2_ref[body, :] = (_layer_norm(x1) * (1.0 + scale2) + shift2).astype(BF16)


def _mix(attn2d, p2d, x2d, mod, mod_row, w_pool, pool_scale, w_out, ln_g, ln_b, seq_len, tm):
    n_tok, d = x2d.shape
    pool_width = p2d.shape[1]
    assert tm % SUB == 0 and seq_len % SUB == 0 and (seq_len % tm == 0 or tm % seq_len == 0)
    halo_per_block = tm // HALO
    n_halo_blocks = n_tok // HALO
    return pl.pallas_call(
        functools.partial(_mix_kernel, tm=tm, seq_len=seq_len),
        grid=(n_tok // tm,),
        in_specs=[
            pl.BlockSpec((tm, ATTN_WIDTH), lambda i: (i, 0)),
            pl.BlockSpec((HALO, pool_width), lambda i: (jnp.maximum(i * halo_per_block - 1, 0), 0)),
            pl.BlockSpec((tm, pool_width), lambda i: (i, 0)),
            pl.BlockSpec((HALO, pool_width),
                         lambda i: (jnp.minimum((i + 1) * halo_per_block, n_halo_blocks - 1), 0)),
            pl.BlockSpec((tm, d), lambda i: (i, 0)),
            pl.BlockSpec((1, N_MOD, d), lambda i: (mod_row(i, tm), 0, 0)),
            _resident(w_pool.shape),
            _resident((1, pool_width)),
            _resident(w_out.shape),
            _resident((1, d)),
            _resident((1, d)),
        ],
        out_specs=[
            pl.BlockSpec((tm, d), lambda i: (i, 0)),
            pl.BlockSpec((tm, d), lambda i: (i, 0)),
        ],
        out_shape=[
            jax.ShapeDtypeStruct((n_tok, d), F32),
            jax.ShapeDtypeStruct((n_tok, d), BF16),
        ],
        compiler_params=_params(("arbitrary",)),
        name="mix",
    )(attn2d, p2d, p2d, p2d, x2d, mod, w_pool, pool_scale, w_out, ln_g, ln_b)


def _mlp_kernel(u_ref, x1_ref, mod_ref, w1_ref, w2_ref, g2_ref, b2_ref, o_ref, acc_ref):
    j = pl.program_id(1)

    @pl.when(j == 0)
    def _():
        acc_ref[...] = jnp.zeros_like(acc_ref)

    h = jnp.maximum(jnp.dot(u_ref[...], w1_ref[...], preferred_element_type=F32), 0.0)
    acc_ref[...] += jnp.dot((h * h).astype(BF16), w2_ref[...], preferred_element_type=F32)

    @pl.when(j == pl.num_programs(1) - 1)
    def _():
        gate2 = mod_ref[0, 5:6, :]
        y = DEEPNORM_ALPHA * x1_ref[...] + gate2 * acc_ref[...]
        o_ref[...] = _layer_norm(y) * g2_ref[...] + b2_ref[...]


def _mlp(u2, x1, mod, mod_row, w1, w2, ln_g, ln_b, tm, tf):
    n_tok, d = x1.shape
    d_ff = w1.shape[1]
    return pl.pallas_call(
        _mlp_kernel,
        grid=(n_tok // tm, d_ff // tf),
        in_specs=[
            pl.BlockSpec((tm, d), lambda i, j: (i, 0)),
            pl.BlockSpec((tm, d), lambda i, j: (i, 0)),
            pl.BlockSpec((1, N_MOD, d), lambda i, j: (mod_row(i, tm), 0, 0)),
            pl.BlockSpec((d, tf), lambda i, j: (0, j)),
            pl.BlockSpec((tf, d), lambda i, j: (j, 0)),
            _resident((1, d)),
            _resident((1, d)),
        ],
        out_specs=pl.BlockSpec((tm, d), lambda i, j: (i, 0)),
        out_shape=jax.ShapeDtypeStruct((n_tok, d), F32),
        scratch_shapes=[pltpu.VMEM((tm, d), F32)],
        compiler_params=_params(("arbitrary", "arbitrary")),
        name="mlp",
    )(u2, x1, mod, w1, w2, ln_g, ln_b)


def _layer(x, mod, mod_row, weights, rope_tables, cache, tm_proj, tq, tm_mix, tm_mlp, tf):
    (w_in, q_gain, k_gain, w_pool, pool_scale, w_out, ln1_g, ln1_b, w_ff1, w_ff2, ln2_g, ln2_b) = weights
    b, t, d = x.shape
    x2d = x.reshape(b * t, d)
    kv_dtype = F32 if cache is None else BF16
    q, k, v, p = _projection(x2d, mod, mod_row, w_in, q_gain, k_gain, rope_tables, t, tm_proj, kv_dtype)
    k3 = k.reshape(b, t, KV_WIDTH)
    v3 = v.reshape(b, t, KV_WIDTH)
    segments = ([] if cache is None else [cache]) + [(k3, v3)]
    attn = _attention(q.reshape(b, t, ATTN_WIDTH), segments, tq)
    x1, u2 = _mix(attn.reshape(b * t, ATTN_WIDTH), p, x2d, mod, mod_row, w_pool, pool_scale, w_out,
                  ln1_g, ln1_b, t, tm_mix)
    y = _mlp(u2, x1, mod, mod_row, w_ff1, w_ff2, ln2_g, ln2_b, tm_mlp, tf)
    return y.reshape(b, t, d), k3, v3


def kernel(x_prompt, x_sample, cache_k, cache_v, c, c_ctx, w_mod, b_mod, w_in, q_gain, k_gain, w_pool,
           pool_scale, w_out, ln1_g, ln1_b, w_ff1, w_ff2, ln2_g, ln2_b):
    assert w_mod.shape[0] == DEPTH and cache_k.shape[1] == DEPTH
    n_ctx, t_ctx, d = x_prompt.shape
    n_lat, t_lat, _ = x_sample.shape
    past = cache_k.shape[2]

    n_rows = -(-(1 + n_lat) // (2 * SUBLANES)) * (2 * SUBLANES)
    cond = jnp.concatenate([c_ctx[None], c, jnp.zeros((n_rows - 1 - n_lat, d), F32)], axis=0)
    mod = _modulation(cond, w_mod[0], b_mod[0]).reshape(n_rows, N_MOD, d)

    row2 = lambda a: a[0].reshape(1, -1)
    weights = (w_in[0].astype(BF16), row2(q_gain), row2(k_gain), w_pool[0].astype(BF16), row2(pool_scale),
               w_out[0].astype(BF16), row2(ln1_g), row2(ln1_b), w_ff1[0].astype(BF16), w_ff2[0].astype(BF16),
               row2(ln2_g), row2(ln2_b))

    y_ctx, k_ctx, v_ctx = _layer(
        x_prompt, mod, lambda i, tm: 0, weights, None, None,
        tm_proj=512, tq=t_ctx, tm_mix=512, tm_mlp=512, tf=1024)

    cache = (cache_k.reshape(n_lat, past, KV_WIDTH), cache_v.reshape(n_lat, past, KV_WIDTH))
    y_lat, _, _ = _layer(
        x_sample, mod, lambda i, tm: 1 + i // (t_lat // tm), weights, _rope_tables(t_lat), cache,
        tm_proj=512, tq=512, tm_mix=512, tm_mlp=512, tf=1024)

    ctx_k = k_ctx.reshape(n_ctx, 1, t_ctx, N_KV_HEADS, HEAD_DIM)
    ctx_v = v_ctx.reshape(n_ctx, 1, t_ctx, N_KV_HEADS, HEAD_DIM)
    return (y_ctx, y_lat, ctx_k, ctx_v)
```

```python
import functools

import jax
import jax.numpy as jnp
from jax import lax
from jax.experimental import pallas as pl
from jax.experimental.pallas import tpu as pltpu

HEAD_DIM = 128
N_Q_HEADS = 8
N_KV_HEADS = 2
Q_PER_KV = N_Q_HEADS // N_KV_HEADS
ATTN_WIDTH = N_Q_HEADS * HEAD_DIM
KV_WIDTH = N_KV_HEADS * HEAD_DIM
POOL_WINDOWS = (2, 4, 8, 16)
N_POOL_GROUPS = len(POOL_WINDOWS)
GRID_W = 64
ROPE_THETA = 10000.0
ROPE_QUARTER = HEAD_DIM // 4
EPS = 1e-6
N_MOD = 6
DEPTH = 1
DEEPNORM_ALPHA = (2.0 * DEPTH) ** 0.25
LOG2_E = 1.4426950408889634
Q_SCALE = LOG2_E * HEAD_DIM ** -0.5
NORM_MARGIN = 1.02
HALO = max(POOL_WINDOWS) // 2
SUBLANES = 8
KEY_CHUNK = 1536
SCORE_LIMIT = 64.0
SUB = 256
MXU_COLS = 256
HEADS_PER_DOT = MXU_COLS // HEAD_DIM
VMEM_LIMIT = 56 * 1024 * 1024

BF16 = jnp.bfloat16
F32 = jnp.float32


def _params(semantics):
    return pltpu.CompilerParams(dimension_semantics=semantics, vmem_limit_bytes=VMEM_LIMIT)


def _resident(shape):
    return pl.BlockSpec(shape, lambda *_: (0,) * len(shape), pipeline_mode=pl.Buffered(1))


def _layer_norm(x):
    mu = jnp.mean(x, axis=-1, keepdims=True)
    xc = x - mu
    var = jnp.mean(xc * xc, axis=-1, keepdims=True)
    return xc * lax.rsqrt(var + EPS)


def _mod_kernel(c_ref, w_ref, b_ref, o_ref):
    c = c_ref[...]
    s = c * (1.0 / (1.0 + jnp.exp(-c)))
    acc = jnp.dot(s.astype(BF16), w_ref[...].astype(BF16), preferred_element_type=F32)
    o_ref[...] = acc + b_ref[...]


def _modulation(cond, w_mod, b_mod, tn=1024):
    rows, d = cond.shape
    n = w_mod.shape[1]
    return pl.pallas_call(
        _mod_kernel,
        grid=(n // tn,),
        in_specs=[
            pl.BlockSpec((rows, d), lambda j: (0, 0)),
            pl.BlockSpec((d, tn), lambda j: (0, j)),
            pl.BlockSpec((1, tn), lambda j: (0, j)),
        ],
        out_specs=pl.BlockSpec((rows, tn), lambda j: (0, j)),
        out_shape=jax.ShapeDtypeStruct((rows, n), F32),
        compiler_params=_params(("arbitrary",)),
        name="modulation",
    )(cond, w_mod, b_mod.reshape(1, n))


def _swap_rope_halves(x):
    lane = lax.broadcasted_iota(jnp.int32, x.shape, 1)
    first = (lane % (2 * ROPE_QUARTER)) < ROPE_QUARTER
    up = pltpu.roll(x, HEAD_DIM - ROPE_QUARTER, 1)
    down = pltpu.roll(x, ROPE_QUARTER, 1)
    return jnp.where(first, up, down)


def _proj_kernel(*refs, rope, q_scale):
    if rope:
        x_ref, mod_ref, w_ref, qg_ref, kg_ref, cos_ref, sin_ref, q_ref, k_ref, v_ref, p_ref = refs
    else:
        x_ref, mod_ref, w_ref, qg_ref, kg_ref, q_ref, k_ref, v_ref, p_ref = refs
    shift = mod_ref[0, 0:1, :]
    scale = mod_ref[0, 1:2, :]
    v0 = ATTN_WIDTH + KV_WIDTH
    p0 = v0 + KV_WIDTH

    for half in range(x_ref.shape[0] // SUB):
        body = slice(half * SUB, (half + 1) * SUB)
        u = (_layer_norm(x_ref[body, :]) * (1.0 + scale) + shift).astype(BF16)

        def heads(out_ref, col0, n_heads, gain_ref, post_scale):
            for first in range(0, n_heads, HEADS_PER_DOT):
                c0 = col0 + first * HEAD_DIM
                raw_all = jnp.dot(u, w_ref[:, c0:c0 + MXU_COLS], preferred_element_type=F32)
                for i in range(HEADS_PER_DOT):
                    raw = raw_all[:, i * HEAD_DIM:(i + 1) * HEAD_DIM]
                    ms = jnp.mean(raw * raw, axis=-1, keepdims=True)
                    y = raw * lax.rsqrt(ms + EPS) * gain_ref[...]
                    if rope:
                        y = y * cos_ref[body, :] + _swap_rope_halves(y) * sin_ref[body, :]
                    if post_scale != 1.0:
                        y = y * post_scale
                    out = (first + i) * HEAD_DIM
                    out_ref[body, out:out + HEAD_DIM] = y.astype(out_ref.dtype)

        heads(q_ref, 0, N_Q_HEADS, qg_ref, q_scale)
        heads(k_ref, ATTN_WIDTH, N_KV_HEADS, kg_ref, 1.0)
        v_ref[body, :] = jnp.dot(u, w_ref[:, v0:v0 + KV_WIDTH], preferred_element_type=F32).astype(v_ref.dtype)
        p_ref[body, :] = jnp.dot(u, w_ref[:, p0:], preferred_element_type=F32).astype(p_ref.dtype)


def _projection(x2d, mod, mod_row, w_in, q_gain, k_gain, rope_tables, seq_len, tm, kv_dtype):
    n_tok, d = x2d.shape
    in_width = w_in.shape[1]
    pool_width = in_width - ATTN_WIDTH - 2 * KV_WIDTH
    rope = rope_tables is not None
    assert tm % SUB == 0
    seq_blocks = seq_len // tm if rope else None
    in_specs = [
        pl.BlockSpec((tm, d), lambda i: (i, 0)),
        pl.BlockSpec((1, N_MOD, d), lambda i: (mod_row(i, tm), 0, 0)),
        _resident((d, in_width)),
        _resident((1, HEAD_DIM)),
        _resident((1, HEAD_DIM)),
    ]
    args = [x2d, mod, w_in, q_gain, k_gain]
    if rope:
        in_specs += [pl.BlockSpec((tm, HEAD_DIM), lambda i: (i % seq_blocks, 0))] * 2
        args += list(rope_tables)
    return pl.pallas_call(
        functools.partial(_proj_kernel, rope=rope, q_scale=Q_SCALE),
        grid=(n_tok // tm,),
        in_specs=in_specs,
        out_specs=[
            pl.BlockSpec((tm, ATTN_WIDTH), lambda i: (i, 0)),
            pl.BlockSpec((tm, KV_WIDTH), lambda i: (i, 0)),
            pl.BlockSpec((tm, KV_WIDTH), lambda i: (i, 0)),
            pl.BlockSpec((tm, pool_width), lambda i: (i, 0)),
        ],
        out_shape=[
            jax.ShapeDtypeStruct((n_tok, ATTN_WIDTH), BF16),
            jax.ShapeDtypeStruct((n_tok, KV_WIDTH), kv_dtype),
            jax.ShapeDtypeStruct((n_tok, KV_WIDTH), kv_dtype),
            jax.ShapeDtypeStruct((n_tok, pool_width), F32),
        ],
        compiler_params=_params(("arbitrary",)),
        name="projection_rope" if rope else "projection",
    )(*args)


def _rope_tables(n_tokens):
    n_rows = n_tokens // GRID_W
    inv_freq = ROPE_THETA ** (-(jnp.arange(ROPE_QUARTER, dtype=F32) / ROPE_QUARTER))
    ang_r = jnp.arange(n_rows, dtype=F32)[:, None] * inv_freq
    ang_c = jnp.arange(GRID_W, dtype=F32)[:, None] * inv_freq
    by_row = lambda a: jnp.repeat(a, GRID_W, axis=0)
    by_col = lambda a: jnp.tile(a, (n_rows, 1))
    cos_r, sin_r = by_row(jnp.cos(ang_r)), by_row(jnp.sin(ang_r))
    cos_c, sin_c = by_col(jnp.cos(ang_c)), by_col(jnp.sin(ang_c))
    cos = jnp.concatenate([cos_r, cos_r, cos_c, cos_c], axis=-1)
    sin = jnp.concatenate([-sin_r, sin_r, -sin_c, sin_c], axis=-1)
    return cos, sin


_NT = (((1,), (1,)), ((), ()))


def _attn_kernel(*refs, n_seg):
    qbound_ref, q_ref = refs[:2]
    k_refs = refs[2:2 + 2 * n_seg:2]
    v_refs = refs[3:3 + 2 * n_seg:2]
    o_ref, k_all, vt_all, s_buf, knorm = refs[2 + 2 * n_seg:]
    n_keys, tq = s_buf.shape[1:]
    chunk = min(KEY_CHUNK, n_keys)
    n_chunks = n_keys // chunk

    @pl.when(pl.program_id(2) == 0)
    def _():
        row = 0
        for k_ref, v_ref in zip(k_refs, v_refs):
            s = k_ref.shape[1]
            k_all[row:row + s, :] = k_ref[0].astype(BF16)
            vt_all[:, row:row + s] = v_ref[0].astype(F32).T.astype(BF16)
            row += s
        kk = k_all[...].astype(F32)
        knorm[0] = jnp.max(jnp.sum(kk * kk, axis=-1, keepdims=True))

    def lanes(g):
        return slice(g * HEAD_DIM, (g + 1) * HEAD_DIM)

    bounded = qbound_ref[0] * knorm[0] <= SCORE_LIMIT * SCORE_LIMIT

    @pl.when(bounded)
    def _():
        _softmax_heads(q_ref, o_ref, k_all, vt_all, s_buf, chunk, n_chunks, lanes, shift=False)

    @pl.when(jnp.logical_not(bounded))
    def _():
        _softmax_heads(q_ref, o_ref, k_all, vt_all, s_buf, chunk, n_chunks, lanes, shift=True)


def _softmax_heads(q_ref, o_ref, k_all, vt_all, s_buf, chunk, n_chunks, lanes, shift):
    tq = s_buf.shape[2]

    def fold(x, op):
        return op(x.reshape(chunk // SUBLANES, SUBLANES, tq), axis=0)

    def score_chunk(g, c, m8):
        rows = slice(c * chunk, (c + 1) * chunk)
        s = lax.dot_general(k_all[rows, :], q_ref[0, :, lanes(g)], _NT, preferred_element_type=F32)
        s_buf[g % 2, rows, :] = s
        if not shift:
            return None
        cm = fold(s, jnp.max)
        return cm if m8 is None else jnp.maximum(m8, cm)

    def value_chunk(g, c, m, l8, acc):
        rows = slice(c * chunk, (c + 1) * chunk)
        s = s_buf[g % 2, rows, :]
        p = jnp.exp2(s - m if shift else s)
        part = jnp.dot(vt_all[:, rows], p.astype(BF16), preferred_element_type=F32)
        ps = fold(p, jnp.sum)
        return (ps, part) if l8 is None else (l8 + ps, acc + part)

    m_prev = None
    for g in range(Q_PER_KV + 1):
        m8, l8, acc = None, None, None
        for c in range(n_chunks):
            if g < Q_PER_KV:
                m8 = score_chunk(g, c, m8)
            if g > 0:
                l8, acc = value_chunk(g - 1, c, m_prev, l8, acc)
        if g > 0:
            denom = jnp.sum(l8, axis=0, keepdims=True)
            o_ref[0, :, lanes(g - 1)] = (acc / denom).T.astype(o_ref.dtype)
        if g < Q_PER_KV and shift:
            m_prev = jnp.max(m8, axis=0, keepdims=True)


def _attention(q, q_bound, segments, tq):
    b, t, _ = q.shape
    group_width = Q_PER_KV * HEAD_DIM
    in_specs = [pl.BlockSpec(memory_space=pltpu.SMEM),
                pl.BlockSpec((1, tq, group_width), lambda bi, h, qi: (bi, qi, h))]
    args = [q_bound, q]
    n_keys = sum(k.shape[1] for k, _ in segments)
    assert n_keys % min(KEY_CHUNK, n_keys) == 0
    for k, v in segments:
        s = k.shape[1]
        in_specs += [pl.BlockSpec((1, s, HEAD_DIM), lambda bi, h, qi: (bi, 0, h))] * 2
        args += [k, v]
    return pl.pallas_call(
        functools.partial(_attn_kernel, n_seg=len(segments)),
        grid=(b, N_KV_HEADS, t // tq),
        in_specs=in_specs,
        out_specs=pl.BlockSpec((1, tq, group_width), lambda bi, h, qi: (bi, qi, h)),
        out_shape=jax.ShapeDtypeStruct((b, t, ATTN_WIDTH), BF16),
        scratch_shapes=[pltpu.VMEM((n_keys, HEAD_DIM), BF16), pltpu.VMEM((HEAD_DIM, n_keys), BF16),
                        pltpu.VMEM((2, n_keys, tq), F32), pltpu.SMEM((1,), F32)],
        compiler_params=_params(("arbitrary", "arbitrary", "arbitrary")),
        name="attention",
    )(*args)


def _window_sum(z, w):
    n = z.shape[0]
    span = 1
    while 2 * span < w:
        z = z + pltpu.roll(z, n - span, 0)
        span *= 2
    return z + pltpu.roll(z, w // 2, 0)


def _mix_kernel(attn_ref, pprev_ref, pcur_ref, pnext_ref, x_ref, mod_ref, wpool_ref, pscale_ref,
                wout_ref, g1_ref, b1_ref, x1_ref, u2_ref, *, tm, seq_len):
    group_width = pcur_ref.shape[1] // N_POOL_GROUPS
    block_pos = (pl.program_id(0) * tm) % seq_len
    n_half = tm // SUB
    gate1 = mod_ref[0, 2:3, :]
    shift2 = mod_ref[0, 3:4, :]
    scale2 = mod_ref[0, 4:5, :]

    for half in range(n_half):
        body = slice(half * SUB, (half + 1) * SUB)
        pos0 = (block_pos + half * SUB) % seq_len
        before = pprev_ref[...] if half == 0 else pcur_ref[half * SUB - HALO:half * SUB, :]
        after = pnext_ref[...] if half == n_half - 1 else pcur_ref[(half + 1) * SUB:(half + 1) * SUB + HALO, :]
        before = jnp.where(pos0 > 0, before, 0.0)
        after = jnp.where(pos0 + SUB < seq_len, after, 0.0)
        cur = pcur_ref[body, :]
        ext = jnp.concatenate([before, cur, after], axis=0)
        pos = pos0 + lax.broadcasted_iota(jnp.int32, (SUB, group_width), 0)
        pieces = [attn_ref[body, :]]
        for g, w in enumerate(POOL_WINDOWS):
            lanes = slice(g * group_width, (g + 1) * group_width)
            total = _window_sum(ext[:, lanes], w)[HALO:HALO + SUB]
            count = jnp.minimum(pos + w // 2, seq_len) - jnp.maximum(pos - w // 2, 0)
            pooled = total / count.astype(F32) - cur[:, lanes]
            mixed = jnp.dot(pooled.astype(BF16), wpool_ref[g], preferred_element_type=F32)
            pieces.append((mixed * pscale_ref[:, lanes]).astype(BF16))
        mix = jnp.dot(jnp.concatenate(pieces, axis=-1), wout_ref[...], preferred_element_type=F32)
        x1 = _layer_norm(DEEPNORM_ALPHA * x_ref[body, :] + gate1 * mix) * g1_ref[...] + b1_ref[...]
        x1_ref[body, :] = x1
        u2_ref[body, :] = (_layer_norm(x1) * (1.0 + scale2) + shift2).astype(BF16)


def _mix(attn2d, p2d, x2d, mod, mod_row, w_pool, pool_scale, w_out, ln_g, ln_b, seq_len, tm):
    n_tok, d = x2d.shape
    pool_width = p2d.shape[1]
    assert tm % SUB == 0 and seq_len % SUB == 0 and (seq_len % tm == 0 or tm % seq_len == 0)
    halo_per_block = tm // HALO
    n_halo_blocks = n_tok // HALO
    return pl.pallas_call(
        functools.partial(_mix_kernel, tm=tm, seq_len=seq_len),
        grid=(n_tok // tm,),
        in_specs=[
            pl.BlockSpec((tm, ATTN_WIDTH), lambda i: (i, 0)),
            pl.BlockSpec((HALO, pool_width), lambda i: (jnp.maximum(i * halo_per_block - 1, 0), 0)),
            pl.BlockSpec((tm, pool_width), lambda i: (i, 0)),
            pl.BlockSpec((HALO, pool_width),
                         lambda i: (jnp.minimum((i + 1) * halo_per_block, n_halo_blocks - 1), 0)),
            pl.BlockSpec((tm, d), lambda i: (i, 0)),
            pl.BlockSpec((1, N_MOD, d), lambda i: (mod_row(i, tm), 0, 0)),
            _resident(w_pool.shape),
            _resident((1, pool_width)),
            _resident(w_out.shape),
            _resident((1, d)),
            _resident((1, d)),
        ],
        out_specs=[
            pl.BlockSpec((tm, d), lambda i: (i, 0)),
            pl.BlockSpec((tm, d), lambda i: (i, 0)),
        ],
        out_shape=[
            jax.ShapeDtypeStruct((n_tok, d), F32),
            jax.ShapeDtypeStruct((n_tok, d), BF16),
        ],
        compiler_params=_params(("arbitrary",)),
        name="mix",
    )(attn2d, p2d, p2d, p2d, x2d, mod, w_pool, pool_scale, w_out, ln_g, ln_b)


def _mlp_kernel(u_ref, x1_ref, mod_ref, w1_ref, w2_ref, g2_ref, b2_ref, o_ref, acc_ref):
    j = pl.program_id(1)

    @pl.when(j == 0)
    def _():
        acc_ref[...] = jnp.zeros_like(acc_ref)

    h = jnp.maximum(jnp.dot(u_ref[...], w1_ref[...], preferred_element_type=F32), 0.0)
    acc_ref[...] += jnp.dot((h * h).astype(BF16), w2_ref[...], preferred_element_type=F32)

    @pl.when(j == pl.num_programs(1) - 1)
    def _():
        gate2 = mod_ref[0, 5:6, :]
        y = DEEPNORM_ALPHA * x1_ref[...] + gate2 * acc_ref[...]
        o_ref[...] = _layer_norm(y) * g2_ref[...] + b2_ref[...]


def _mlp(u2, x1, mod, mod_row, w1, w2, ln_g, ln_b, tm, tf):
    n_tok, d = x1.shape
    d_ff = w1.shape[1]
    return pl.pallas_call(
        _mlp_kernel,
        grid=(n_tok // tm, d_ff // tf),
        in_specs=[
            pl.BlockSpec((tm, d), lambda i, j: (i, 0)),
            pl.BlockSpec((tm, d), lambda i, j: (i, 0)),
            pl.BlockSpec((1, N_MOD, d), lambda i, j: (mod_row(i, tm), 0, 0)),
            pl.BlockSpec((d, tf), lambda i, j: (0, j)),
            pl.BlockSpec((tf, d), lambda i, j: (j, 0)),
            _resident((1, d)),
            _resident((1, d)),
        ],
        out_specs=pl.BlockSpec((tm, d), lambda i, j: (i, 0)),
        out_shape=jax.ShapeDtypeStruct((n_tok, d), F32),
        scratch_shapes=[pltpu.VMEM((tm, d), F32)],
        compiler_params=_params(("arbitrary", "arbitrary")),
        name="mlp",
    )(u2, x1, mod, w1, w2, ln_g, ln_b)


def _layer(x, mod, mod_row, weights, rope_tables, cache, tm_proj, tq, tm_mix, tm_mlp, tf):
    (w_in, q_gain, k_gain, w_pool, pool_scale, w_out, ln1_g, ln1_b, w_ff1, w_ff2, ln2_g, ln2_b) = weights
    b, t, d = x.shape
    x2d = x.reshape(b * t, d)
    kv_dtype = F32 if cache is None else BF16
    q, k, v, p = _projection(x2d, mod, mod_row, w_in, q_gain, k_gain, rope_tables, t, tm_proj, kv_dtype)
    k3 = k.reshape(b, t, KV_WIDTH)
    v3 = v.reshape(b, t, KV_WIDTH)
    segments = ([] if cache is None else [cache]) + [(k3, v3)]
    q_bound = (HEAD_DIM * Q_SCALE ** 2 * NORM_MARGIN * jnp.max(q_gain * q_gain)).reshape(1)
    attn = _attention(q.reshape(b, t, ATTN_WIDTH), q_bound, segments, tq)
    x1, u2 = _mix(attn.reshape(b * t, ATTN_WIDTH), p, x2d, mod, mod_row, w_pool, pool_scale, w_out,
                  ln1_g, ln1_b, t, tm_mix)
    y = _mlp(u2, x1, mod, mod_row, w_ff1, w_ff2, ln2_g, ln2_b, tm_mlp, tf)
    return y.reshape(b, t, d), k3, v3


def kernel(x_prompt, x_sample, cache_k, cache_v, c, c_ctx, w_mod, b_mod, w_in, q_gain, k_gain, w_pool,
           pool_scale, w_out, ln1_g, ln1_b, w_ff1, w_ff2, ln2_g, ln2_b):
    assert w_mod.shape[0] == DEPTH and cache_k.shape[1] == DEPTH
    n_ctx, t_ctx, d = x_prompt.shape
    n_lat, t_lat, _ = x_sample.shape
    past = cache_k.shape[2]

    n_rows = -(-(1 + n_lat) // (2 * SUBLANES)) * (2 * SUBLANES)
    cond = jnp.concatenate([c_ctx[None], c, jnp.zeros((n_rows - 1 - n_lat, d), F32)], axis=0)
    mod = _modulation(cond, w_mod[0], b_mod[0]).reshape(n_rows, N_MOD, d)

    row2 = lambda a: a[0].reshape(1, -1)
    weights = (w_in[0].astype(BF16), row2(q_gain), row2(k_gain), w_pool[0].astype(BF16), row2(pool_scale),
               w_out[0].astype(BF16), row2(ln1_g), row2(ln1_b), w_ff1[0].astype(BF16), w_ff2[0].astype(BF16),
               row2(ln2_g), row2(ln2_b))

    y_ctx, k_ctx, v_ctx = _layer(
        x_prompt, mod, lambda i, tm: 0, weights, None, None,
        tm_proj=512, tq=t_ctx, tm_mix=512, tm_mlp=512, tf=1024)

    cache = (cache_k.reshape(n_lat, past, KV_WIDTH), cache_v.reshape(n_lat, past, KV_WIDTH))
    y_lat, _, _ = _layer(
        x_sample, mod, lambda i, tm: 1 + i // (t_lat // tm), weights, _rope_tables(t_lat), cache,
        tm_proj=512, tq=512, tm_mix=512, tm_mlp=512, tf=1024)

    ctx_k = k_ctx.reshape(n_ctx, 1, t_ctx, N_KV_HEADS, HEAD_DIM)
    ctx_v = v_ctx.reshape(n_ctx, 1, t_ctx, N_KV_HEADS, HEAD_DIM)
    return (y_ctx, y_lat, ctx_k, ctx_v)
```

```python
import functools

import jax
import jax.numpy as jnp
from jax import lax
from jax.experimental import pallas as pl
from jax.experimental.pallas import tpu as pltpu

HEAD_DIM = 128
N_Q_HEADS = 8
N_KV_HEADS = 2
Q_PER_KV = N_Q_HEADS // N_KV_HEADS
ATTN_WIDTH = N_Q_HEADS * HEAD_DIM
KV_WIDTH = N_KV_HEADS * HEAD_DIM
POOL_WINDOWS = (2, 4, 8, 16)
N_POOL_GROUPS = len(POOL_WINDOWS)
GRID_W = 64
ROPE_THETA = 10000.0
ROPE_QUARTER = HEAD_DIM // 4
EPS = 1e-6
N_MOD = 6
DEPTH = 1
DEEPNORM_ALPHA = (2.0 * DEPTH) ** 0.25
LOG2_E = 1.4426950408889634
Q_SCALE = LOG2_E * HEAD_DIM ** -0.5
NORM_MARGIN = 1.02
HALO = max(POOL_WINDOWS) // 2
SUBLANES = 8
KEY_CHUNK = 1536
SCORE_LIMIT = 64.0
SUB = 256
MXU_COLS = 256
HEADS_PER_DOT = MXU_COLS // HEAD_DIM
VMEM_LIMIT = 56 * 1024 * 1024

BF16 = jnp.bfloat16
F32 = jnp.float32


def _params(semantics):
    return pltpu.CompilerParams(dimension_semantics=semantics, vmem_limit_bytes=VMEM_LIMIT)


def _resident(shape):
    return pl.BlockSpec(shape, lambda *_: (0,) * len(shape), pipeline_mode=pl.Buffered(1))


def _layer_norm(x):
    mu = jnp.mean(x, axis=-1, keepdims=True)
    xc = x - mu
    var = jnp.mean(xc * xc, axis=-1, keepdims=True)
    return xc * lax.rsqrt(var + EPS)


def _mod_kernel(c_ref, w_ref, b_ref, o_ref):
    c = c_ref[...]
    s = c * (1.0 / (1.0 + jnp.exp(-c)))
    acc = jnp.dot(s.astype(BF16), w_ref[...].astype(BF16), preferred_element_type=F32)
    o_ref[...] = acc + b_ref[...]


def _modulation(cond, w_mod, b_mod, tn=1024):
    rows, d = cond.shape
    n = w_mod.shape[1]
    return pl.pallas_call(
        _mod_kernel,
        grid=(n // tn,),
        in_specs=[
            pl.BlockSpec((rows, d), lambda j: (0, 0)),
            pl.BlockSpec((d, tn), lambda j: (0, j)),
            pl.BlockSpec((1, tn), lambda j: (0, j)),
        ],
        out_specs=pl.BlockSpec((rows, tn), lambda j: (0, j)),
        out_shape=jax.ShapeDtypeStruct((rows, n), F32),
        compiler_params=_params(("arbitrary",)),
        name="modulation",
    )(cond, w_mod, b_mod.reshape(1, n))


def _swap_rope_halves(x):
    lane = lax.broadcasted_iota(jnp.int32, x.shape, 1)
    first = (lane % (2 * ROPE_QUARTER)) < ROPE_QUARTER
    up = pltpu.roll(x, HEAD_DIM - ROPE_QUARTER, 1)
    down = pltpu.roll(x, ROPE_QUARTER, 1)
    return jnp.where(first, up, down)


def _proj_kernel(*refs, rope, q_scale):
    if rope:
        x_ref, mod_ref, w_ref, qg_ref, kg_ref, cos_ref, sin_ref, q_ref, k_ref, v_ref, p_ref = refs
    else:
        x_ref, mod_ref, w_ref, qg_ref, kg_ref, q_ref, k_ref, v_ref, p_ref = refs
    shift = mod_ref[0, 0:1, :]
    scale = mod_ref[0, 1:2, :]
    v0 = ATTN_WIDTH + KV_WIDTH
    p0 = v0 + KV_WIDTH

    for half in range(x_ref.shape[0] // SUB):
        body = slice(half * SUB, (half + 1) * SUB)
        u = (_layer_norm(x_ref[body, :]) * (1.0 + scale) + shift).astype(BF16)

        def heads(out_ref, col0, n_heads, gain_ref, post_scale):
            for first in range(0, n_heads, HEADS_PER_DOT):
                c0 = col0 + first * HEAD_DIM
                raw_all = jnp.dot(u, w_ref[:, c0:c0 + MXU_COLS], preferred_element_type=F32)
                for i in range(HEADS_PER_DOT):
                    raw = raw_all[:, i * HEAD_DIM:(i + 1) * HEAD_DIM]
                    ms = jnp.mean(raw * raw, axis=-1, keepdims=True)
                    y = raw * lax.rsqrt(ms + EPS) * gain_ref[...]
                    if rope:
                        y = y * cos_ref[body, :] + _swap_rope_halves(y) * sin_ref[body, :]
                    if post_scale != 1.0:
                        y = y * post_scale
                    out = (first + i) * HEAD_DIM
                    out_ref[body, out:out + HEAD_DIM] = y.astype(out_ref.dtype)

        heads(q_ref, 0, N_Q_HEADS, qg_ref, q_scale)
        heads(k_ref, ATTN_WIDTH, N_KV_HEADS, kg_ref, 1.0)
        v_ref[body, :] = jnp.dot(u, w_ref[:, v0:v0 + KV_WIDTH], preferred_element_type=F32).astype(v_ref.dtype)
        p_ref[body, :] = jnp.dot(u, w_ref[:, p0:], preferred_element_type=F32).astype(p_ref.dtype)


def _projection(x2d, mod, mod_row, w_in, q_gain, k_gain, rope_tables, seq_len, tm, kv_dtype):
    n_tok, d = x2d.shape
    in_width = w_in.shape[1]
    pool_width = in_width - ATTN_WIDTH - 2 * KV_WIDTH
    rope = rope_tables is not None
    assert tm % SUB == 0
    seq_blocks = seq_len // tm if rope else None
    in_specs = [
        pl.BlockSpec((tm, d), lambda i: (i, 0)),
        pl.BlockSpec((1, N_MOD, d), lambda i: (mod_row(i, tm), 0, 0)),
        _resident((d, in_width)),
        _resident((1, HEAD_DIM)),
        _resident((1, HEAD_DIM)),
    ]
    args = [x2d, mod, w_in, q_gain, k_gain]
    if rope:
        in_specs += [pl.BlockSpec((tm, HEAD_DIM), lambda i: (i % seq_blocks, 0))] * 2
        args += list(rope_tables)
    return pl.pallas_call(
        functools.partial(_proj_kernel, rope=rope, q_scale=Q_SCALE),
        grid=(n_tok // tm,),
        in_specs=in_specs,
        out_specs=[
            pl.BlockSpec((tm, ATTN_WIDTH), lambda i: (i, 0)),
            pl.BlockSpec((tm, KV_WIDTH), lambda i: (i, 0)),
            pl.BlockSpec((tm, KV_WIDTH), lambda i: (i, 0)),
            pl.BlockSpec((tm, pool_width), lambda i: (i, 0)),
        ],
        out_shape=[
            jax.ShapeDtypeStruct((n_tok, ATTN_WIDTH), BF16),
            jax.ShapeDtypeStruct((n_tok, KV_WIDTH), kv_dtype),
            jax.ShapeDtypeStruct((n_tok, KV_WIDTH), kv_dtype),
            jax.ShapeDtypeStruct((n_tok, pool_width), F32),
        ],
        compiler_params=_params(("arbitrary",)),
        name="projection_rope" if rope else "projection",
    )(*args)


def _rope_tables(n_tokens):
    n_rows = n_tokens // GRID_W
    inv_freq = ROPE_THETA ** (-(jnp.arange(ROPE_QUARTER, dtype=F32) / ROPE_QUARTER))
    ang_r = jnp.arange(n_rows, dtype=F32)[:, None] * inv_freq
    ang_c = jnp.arange(GRID_W, dtype=F32)[:, None] * inv_freq
    by_row = lambda a: jnp.repeat(a, GRID_W, axis=0)
    by_col = lambda a: jnp.tile(a, (n_rows, 1))
    cos_r, sin_r = by_row(jnp.cos(ang_r)), by_row(jnp.sin(ang_r))
    cos_c, sin_c = by_col(jnp.cos(ang_c)), by_col(jnp.sin(ang_c))
    cos = jnp.concatenate([cos_r, cos_r, cos_c, cos_c], axis=-1)
    sin = jnp.concatenate([-sin_r, sin_r, -sin_c, sin_c], axis=-1)
    return cos, sin


_NT = (((1,), (1,)), ((), ()))


def _attn_kernel(*refs, n_seg):
    qbound_ref, q_ref = refs[:2]
    k_refs = refs[2:2 + 2 * n_seg:2]
    v_refs = refs[3:3 + 2 * n_seg:2]
    o_ref, k_all, vt_all, s_buf, knorm = refs[2 + 2 * n_seg:]
    n_keys, tq = s_buf.shape[1:]
    chunk = min(KEY_CHUNK, n_keys)
    n_chunks = n_keys // chunk

    @pl.when(pl.program_id(2) == 0)
    def _():
        row = 0
        for k_ref, v_ref in zip(k_refs, v_refs):
            s = k_ref.shape[1]
            k_all[row:row + s, :] = k_ref[0].astype(BF16)
            vt_all[:, row:row + s] = v_ref[0].astype(F32).T.astype(BF16)
            row += s
        kk = k_all[...].astype(F32)
        knorm[0] = jnp.max(jnp.sum(kk * kk, axis=-1, keepdims=True))

    def lanes(g):
        return slice(g * HEAD_DIM, (g + 1) * HEAD_DIM)

    bounded = qbound_ref[0] * knorm[0] <= SCORE_LIMIT * SCORE_LIMIT

    @pl.when(bounded)
    def _():
        _softmax_heads(q_ref, o_ref, k_all, vt_all, s_buf, chunk, n_chunks, lanes, shift=False)

    @pl.when(jnp.logical_not(bounded))
    def _():
        _softmax_heads(q_ref, o_ref, k_all, vt_all, s_buf, chunk, n_chunks, lanes, shift=True)


def _softmax_heads(q_ref, o_ref, k_all, vt_all, s_buf, chunk, n_chunks, lanes, shift):
    tq = s_buf.shape[2]

    def fold(x, op):
        return op(x.reshape(chunk // SUBLANES, SUBLANES, tq), axis=0)

    def score_chunk(g, c, m8):
        rows = slice(c * chunk, (c + 1) * chunk)
        s = lax.dot_general(k_all[rows, :], q_ref[0, :, lanes(g)], _NT, preferred_element_type=F32)
        s_buf[g % 2, rows, :] = s
        if not shift:
            return None
        cm = fold(s, jnp.max)
        return cm if m8 is None else jnp.maximum(m8, cm)

    def value_chunk(g, c, m, l8, acc):
        rows = slice(c * chunk, (c + 1) * chunk)
        s = s_buf[g % 2, rows, :]
        p = jnp.exp2(s - m if shift else s)
        part = jnp.dot(vt_all[:, rows], p.astype(BF16), preferred_element_type=F32)
        ps = fold(p, jnp.sum)
        return (ps, part) if l8 is None else (l8 + ps, acc + part)

    m_prev = None
    for g in range(Q_PER_KV + 1):
        m8, l8, acc = None, None, None
        for c in range(n_chunks):
            if g < Q_PER_KV:
                m8 = score_chunk(g, c, m8)
            if g > 0:
                l8, acc = value_chunk(g - 1, c, m_prev, l8, acc)
        if g > 0:
            denom = jnp.sum(l8, axis=0, keepdims=True)
            o_ref[0, :, lanes(g - 1)] = (acc / denom).T.astype(o_ref.dtype)
        if g < Q_PER_KV and shift:
            m_prev = jnp.max(m8, axis=0, keepdims=True)


def _attention(q, q_bound, segments, tq):
    b, t, _ = q.shape
    group_width = Q_PER_KV * HEAD_DIM
    in_specs = [pl.BlockSpec(memory_space=pltpu.SMEM),
                pl.BlockSpec((1, tq, group_width), lambda bi, h, qi: (bi, qi, h))]
    args = [q_bound, q]
    n_keys = sum(k.shape[1] for k, _ in segments)
    assert n_keys % min(KEY_CHUNK, n_keys) == 0
    for k, v in segments:
        s = k.shape[1]
        in_specs += [pl.BlockSpec((1, s, HEAD_DIM), lambda bi, h, qi: (bi, 0, h))] * 2
        args += [k, v]
    return pl.pallas_call(
        functools.partial(_attn_kernel, n_seg=len(segments)),
        grid=(b, N_KV_HEADS, t // tq),
        in_specs=in_specs,
        out_specs=pl.BlockSpec((1, tq, group_width), lambda bi, h, qi: (bi, qi, h)),
        out_shape=jax.ShapeDtypeStruct((b, t, ATTN_WIDTH), BF16),
        scratch_shapes=[pltpu.VMEM((n_keys, HEAD_DIM), BF16), pltpu.VMEM((HEAD_DIM, n_keys), BF16),
                        pltpu.VMEM((2, n_keys, tq), F32), pltpu.SMEM((1,), F32)],
        compiler_params=_params(("arbitrary", "arbitrary", "arbitrary")),
        name="attention",
    )(*args)


def _window_sum(z, w):
    n = z.shape[0]
    span = 1
    while 2 * span < w:
        z = z + pltpu.roll(z, n - span, 0)
        span *= 2
    return z + pltpu.roll(z, w // 2, 0)


def _mix_kernel(attn_ref, pprev_ref, pcur_ref, pnext_ref, x_ref, mod_ref, wpool_ref, pscale_ref,
                wout_ref, g1_ref, b1_ref, x1_ref, u2_ref, *, tm, seq_len):
    group_width = pcur_ref.shape[1] // N_POOL_GROUPS
    block_pos = (pl.program_id(0) * tm) % seq_len
    n_half = tm // SUB
    gate1 = mod_ref[0, 2:3, :]
    shift2 = mod_ref[0, 3:4, :]
    scale2 = mod_ref[0, 4:5, :]

    for half in range(n_half):
        body = slice(half * SUB, (half + 1) * SUB)
        pos0 = (block_pos + half * SUB) % seq_len
        before = pprev_ref[...] if half == 0 else pcur_ref[half * SUB - HALO:half * SUB, :]
        after = pnext_ref[...] if half == n_half - 1 else pcur_ref[(half + 1) * SUB:(half + 1) * SUB + HALO, :]
        before = jnp.where(pos0 > 0, before, 0.0)
        after = jnp.where(pos0 + SUB < seq_len, after, 0.0)
        cur = pcur_ref[body, :]
        ext = jnp.concatenate([before, cur, after], axis=0)
        pos = pos0 + lax.broadcasted_iota(jnp.int32, (SUB, group_width), 0)
        pieces = [attn_ref[body, :]]
        for g, w in enumerate(POOL_WINDOWS):
            lanes = slice(g * group_width, (g + 1) * group_width)
            total = _window_sum(ext[:, lanes], w)[HALO:HALO + SUB]
            count = jnp.minimum(pos + w // 2, seq_len) - jnp.maximum(pos - w // 2, 0)
            pooled = total / count.astype(F32) - cur[:, lanes]
            mixed = jnp.dot(pooled.astype(BF16), wpool_ref[g], preferred_element_type=F32)
            pieces.append((mixed * pscale_ref[:, lanes]).astype(BF16))
        mix = jnp.dot(jnp.concatenate(pieces, axis=-1), wout_ref[...], preferred_element_type=F32)
        x1 = _layer_norm(DEEPNORM_ALPHA * x_ref[body, :] + gate1 * mix) * g1_ref[...] + b1_ref[...]
        x1_ref[body, :] = x1
        u2_ref[body, :] = (_layer_norm(x1) * (1.0 + scale2) + shift2).astype(BF16)


def _mix(attn2d, p2d, x2d, mod, mod_row, w_pool, pool_scale, w_out, ln_g, ln_b, seq_len, tm):
    n_tok, d = x2d.shape
    pool_width = p2d.shape[1]
    assert tm % SUB == 0 and seq_len % SUB == 0 and (seq_len % tm == 0 or tm % seq_len == 0)
    halo_per_block = tm // HALO
    n_halo_blocks = n_tok // HALO
    return pl.pallas_call(
        functools.partial(_mix_kernel, tm=tm, seq_len=seq_len),
        grid=(n_tok // tm,),
        in_specs=[
            pl.BlockSpec((tm, ATTN_WIDTH), lambda i: (i, 0)),
            pl.BlockSpec((HALO, pool_width), lambda i: (jnp.maximum(i * halo_per_block - 1, 0), 0)),
            pl.BlockSpec((tm, pool_width), lambda i: (i, 0)),
            pl.BlockSpec((HALO, pool_width),
                         lambda i: (jnp.minimum((i + 1) * halo_per_block, n_halo_blocks - 1), 0)),
            pl.BlockSpec((tm, d), lambda i: (i, 0)),
            pl.BlockSpec((1, N_MOD, d), lambda i: (mod_row(i, tm), 0, 0)),
            _resident(w_pool.shape),
            _resident((1, pool_width)),
            _resident(w_out.shape),
            _resident((1, d)),
            _resident((1, d)),
        ],
        out_specs=[
            pl.BlockSpec((tm, d), lambda i: (i, 0)),
            pl.BlockSpec((tm, d), lambda i: (i, 0)),
        ],
        out_shape=[
            jax.ShapeDtypeStruct((n_tok, d), F32),
            jax.ShapeDtypeStruct((n_tok, d), BF16),
        ],
        compiler_params=_params(("arbitrary",)),
        name="mix",
    )(attn2d, p2d, p2d, p2d, x2d, mod, w_pool, pool_scale, w_out, ln_g, ln_b)


def _mlp_kernel(u_ref, x1_ref, mod_ref, w1_ref, w2_ref, g2_ref, b2_ref, o_ref, acc_ref):
    j = pl.program_id(1)
    last = pl.num_programs(1) - 1

    def slab():
        h = jnp.maximum(jnp.dot(u_ref[...], w1_ref[...], preferred_element_type=F32), 0.0)
        return jnp.dot((h * h).astype(BF16), w2_ref[...], preferred_element_type=F32)

    @pl.when(j == 0)
    def _():
        acc_ref[...] = slab()

    @pl.when(jnp.logical_and(j > 0, j < last))
    def _():
        acc_ref[...] += slab()

    @pl.when(j == last)
    def _():
        gate2 = mod_ref[0, 5:6, :]
        y = DEEPNORM_ALPHA * x1_ref[...] + gate2 * (acc_ref[...] + slab())
        o_ref[...] = _layer_norm(y) * g2_ref[...] + b2_ref[...]


def _mlp(u2, x1, mod, mod_row, w1, w2, ln_g, ln_b, tm, tf):
    n_tok, d = x1.shape
    d_ff = w1.shape[1]
    assert d_ff // tf >= 2, "the kernel needs distinct first and last hidden slabs"
    return pl.pallas_call(
        _mlp_kernel,
        grid=(n_tok // tm, d_ff // tf),
        in_specs=[
            pl.BlockSpec((tm, d), lambda i, j: (i, 0)),
            pl.BlockSpec((tm, d), lambda i, j: (i, 0)),
            pl.BlockSpec((1, N_MOD, d), lambda i, j: (mod_row(i, tm), 0, 0)),
            pl.BlockSpec((d, tf), lambda i, j: (0, j)),
            pl.BlockSpec((tf, d), lambda i, j: (j, 0)),
            _resident((1, d)),
            _resident((1, d)),
        ],
        out_specs=pl.BlockSpec((tm, d), lambda i, j: (i, 0)),
        out_shape=jax.ShapeDtypeStruct((n_tok, d), F32),
        scratch_shapes=[pltpu.VMEM((tm, d), F32)],
        compiler_params=_params(("arbitrary", "arbitrary")),
        name="mlp",
    )(u2, x1, mod, w1, w2, ln_g, ln_b)


def _layer(x, mod, mod_row, weights, rope_tables, cache, tm_proj, tq, tm_mix, tm_mlp, tf):
    (w_in, q_gain, k_gain, w_pool, pool_scale, w_out, ln1_g, ln1_b, w_ff1, w_ff2, ln2_g, ln2_b) = weights
    b, t, d = x.shape
    x2d = x.reshape(b * t, d)
    kv_dtype = F32 if cache is None else BF16
    q, k, v, p = _projection(x2d, mod, mod_row, w_in, q_gain, k_gain, rope_tables, t, tm_proj, kv_dtype)
    k3 = k.reshape(b, t, KV_WIDTH)
    v3 = v.reshape(b, t, KV_WIDTH)
    segments = ([] if cache is None else [cache]) + [(k3, v3)]
    q_bound = (HEAD_DIM * Q_SCALE ** 2 * NORM_MARGIN * jnp.max(q_gain * q_gain)).reshape(1)
    attn = _attention(q.reshape(b, t, ATTN_WIDTH), q_bound, segments, tq)
    x1, u2 = _mix(attn.reshape(b * t, ATTN_WIDTH), p, x2d, mod, mod_row, w_pool, pool_scale, w_out,
                  ln1_g, ln1_b, t, tm_mix)
    y = _mlp(u2, x1, mod, mod_row, w_ff1, w_ff2, ln2_g, ln2_b, tm_mlp, tf)
    return y.reshape(b, t, d), k3, v3


def kernel(x_prompt, x_sample, cache_k, cache_v, c, c_ctx, w_mod, b_mod, w_in, q_gain, k_gain, w_pool,
           pool_scale, w_out, ln1_g, ln1_b, w_ff1, w_ff2, ln2_g, ln2_b):
    assert w_mod.shape[0] == DEPTH and cache_k.shape[1] == DEPTH
    n_ctx, t_ctx, d = x_prompt.shape
    n_lat, t_lat, _ = x_sample.shape
    past = cache_k.shape[2]

    n_rows = -(-(1 + n_lat) // (2 * SUBLANES)) * (2 * SUBLANES)
    cond = jnp.concatenate([c_ctx[None], c, jnp.zeros((n_rows - 1 - n_lat, d), F32)], axis=0)
    mod = _modulation(cond, w_mod[0], b_mod[0]).reshape(n_rows, N_MOD, d)

    row2 = lambda a: a[0].reshape(1, -1)
    weights = (w_in[0].astype(BF16), row2(q_gain), row2(k_gain), w_pool[0].astype(BF16), row2(pool_scale),
               w_out[0].astype(BF16), row2(ln1_g), row2(ln1_b), w_ff1[0].astype(BF16), w_ff2[0].astype(BF16),
               row2(ln2_g), row2(ln2_b))

    y_ctx, k_ctx, v_ctx = _layer(
        x_prompt, mod, lambda i, tm: 0, weights, None, None,
        tm_proj=512, tq=t_ctx, tm_mix=512, tm_mlp=512, tf=1024)

    cache = (cache_k.reshape(n_lat, past, KV_WIDTH), cache_v.reshape(n_lat, past, KV_WIDTH))
    y_lat, _, _ = _layer(
        x_sample, mod, lambda i, tm: 1 + i // (t_lat // tm), weights, _rope_tables(t_lat), cache,
        tm_proj=512, tq=512, tm_mix=512, tm_mlp=512, tf=1024)

    ctx_k = k_ctx.reshape(n_ctx, 1, t_ctx, N_KV_HEADS, HEAD_DIM)
    ctx_v = v_ctx.reshape(n_ctx, 1, t_ctx, N_KV_HEADS, HEAD_DIM)
    return (y_ctx, y_lat, ctx_k, ctx_v)
```

```python
import functools

import jax
import jax.numpy as jnp
from jax import lax
from jax.experimental import pallas as pl
from jax.experimental.pallas import tpu as pltpu

HEAD_DIM = 128
N_Q_HEADS = 8
N_KV_HEADS = 2
Q_PER_KV = N_Q_HEADS // N_KV_HEADS
ATTN_WIDTH = N_Q_HEADS * HEAD_DIM
KV_WIDTH = N_KV_HEADS * HEAD_DIM
POOL_WINDOWS = (2, 4, 8, 16)
N_POOL_GROUPS = len(POOL_WINDOWS)
GRID_W = 64
ROPE_THETA = 10000.0
ROPE_QUARTER = HEAD_DIM // 4
EPS = 1e-6
N_MOD = 6
DEPTH = 1
DEEPNORM_ALPHA = (2.0 * DEPTH) ** 0.25
LOG2_E = 1.4426950408889634
Q_SCALE = LOG2_E * HEAD_DIM ** -0.5
NORM_MARGIN = 1.02
HALO = max(POOL_WINDOWS) // 2
SUBLANES = 8
KEY_CHUNK = 1536
TQ_SUB = 512
SCORE_LIMIT = 64.0
SUB = 256
MXU_COLS = 256
HEADS_PER_DOT = MXU_COLS // HEAD_DIM
VMEM_LIMIT = 56 * 1024 * 1024

BF16 = jnp.bfloat16
F32 = jnp.float32


def _params(semantics):
    return pltpu.CompilerParams(dimension_semantics=semantics, vmem_limit_bytes=VMEM_LIMIT)


def _resident(shape):
    return pl.BlockSpec(shape, lambda *_: (0,) * len(shape), pipeline_mode=pl.Buffered(1))


def _layer_norm(x):
    mu = jnp.mean(x, axis=-1, keepdims=True)
    xc = x - mu
    var = jnp.mean(xc * xc, axis=-1, keepdims=True)
    return xc * lax.rsqrt(var + EPS)


def _mod_kernel(c_ref, w_ref, b_ref, o_ref):
    c = c_ref[...]
    s = c * (1.0 / (1.0 + jnp.exp(-c)))
    acc = jnp.dot(s.astype(BF16), w_ref[...].astype(BF16), preferred_element_type=F32)
    o_ref[...] = acc + b_ref[...]


def _modulation(cond, w_mod, b_mod, tn=1024):
    rows, d = cond.shape
    n = w_mod.shape[1]
    return pl.pallas_call(
        _mod_kernel,
        grid=(n // tn,),
        in_specs=[
            pl.BlockSpec((rows, d), lambda j: (0, 0)),
            pl.BlockSpec((d, tn), lambda j: (0, j)),
            pl.BlockSpec((1, tn), lambda j: (0, j)),
        ],
        out_specs=pl.BlockSpec((rows, tn), lambda j: (0, j)),
        out_shape=jax.ShapeDtypeStruct((rows, n), F32),
        compiler_params=_params(("arbitrary",)),
        name="modulation",
    )(cond, w_mod, b_mod.reshape(1, n))


def _swap_rope_halves(x):
    lane = lax.broadcasted_iota(jnp.int32, x.shape, 1)
    first = (lane % (2 * ROPE_QUARTER)) < ROPE_QUARTER
    up = pltpu.roll(x, HEAD_DIM - ROPE_QUARTER, 1)
    down = pltpu.roll(x, ROPE_QUARTER, 1)
    return jnp.where(first, up, down)


def _proj_kernel(*refs, rope, q_scale):
    if rope:
        x_ref, mod_ref, w_ref, qg_ref, kg_ref, cos_ref, sin_ref, q_ref, k_ref, v_ref, p_ref = refs
    else:
        x_ref, mod_ref, w_ref, qg_ref, kg_ref, q_ref, k_ref, v_ref, p_ref = refs
    shift = mod_ref[0, 0:1, :]
    scale = mod_ref[0, 1:2, :]
    v0 = ATTN_WIDTH + KV_WIDTH
    p0 = v0 + KV_WIDTH

    for half in range(x_ref.shape[0] // SUB):
        body = slice(half * SUB, (half + 1) * SUB)
        u = (_layer_norm(x_ref[body, :]) * (1.0 + scale) + shift).astype(BF16)

        def heads(out_ref, col0, n_heads, gain_ref, post_scale):
            for first in range(0, n_heads, HEADS_PER_DOT):
                c0 = col0 + first * HEAD_DIM
                raw_all = jnp.dot(u, w_ref[:, c0:c0 + MXU_COLS], preferred_element_type=F32)
                for i in range(HEADS_PER_DOT):
                    raw = raw_all[:, i * HEAD_DIM:(i + 1) * HEAD_DIM]
                    ms = jnp.mean(raw * raw, axis=-1, keepdims=True)
                    y = raw * lax.rsqrt(ms + EPS) * gain_ref[...]
                    if rope:
                        y = y * cos_ref[body, :] + _swap_rope_halves(y) * sin_ref[body, :]
                    if post_scale != 1.0:
                        y = y * post_scale
                    out = (first + i) * HEAD_DIM
                    out_ref[body, out:out + HEAD_DIM] = y.astype(out_ref.dtype)

        heads(q_ref, 0, N_Q_HEADS, qg_ref, q_scale)
        heads(k_ref, ATTN_WIDTH, N_KV_HEADS, kg_ref, 1.0)
        v_ref[body, :] = jnp.dot(u, w_ref[:, v0:v0 + KV_WIDTH], preferred_element_type=F32).astype(v_ref.dtype)
        p_ref[body, :] = jnp.dot(u, w_ref[:, p0:], preferred_element_type=F32).astype(p_ref.dtype)


def _projection(x2d, mod, mod_row, w_in, q_gain, k_gain, rope_tables, seq_len, tm, kv_dtype):
    n_tok, d = x2d.shape
    in_width = w_in.shape[1]
    pool_width = in_width - ATTN_WIDTH - 2 * KV_WIDTH
    rope = rope_tables is not None
    assert tm % SUB == 0
    seq_blocks = seq_len // tm if rope else None
    in_specs = [
        pl.BlockSpec((tm, d), lambda i: (i, 0)),
        pl.BlockSpec((1, N_MOD, d), lambda i: (mod_row(i, tm), 0, 0)),
        _resident((d, in_width)),
        _resident((1, HEAD_DIM)),
        _resident((1, HEAD_DIM)),
    ]
    args = [x2d, mod, w_in, q_gain, k_gain]
    if rope:
        in_specs += [pl.BlockSpec((tm, HEAD_DIM), lambda i: (i % seq_blocks, 0))] * 2
        args += list(rope_tables)
    return pl.pallas_call(
        functools.partial(_proj_kernel, rope=rope, q_scale=Q_SCALE),
        grid=(n_tok // tm,),
        in_specs=in_specs,
        out_specs=[
            pl.BlockSpec((tm, ATTN_WIDTH), lambda i: (i, 0)),
            pl.BlockSpec((tm, KV_WIDTH), lambda i: (i, 0)),
            pl.BlockSpec((tm, KV_WIDTH), lambda i: (i, 0)),
            pl.BlockSpec((tm, pool_width), lambda i: (i, 0)),
        ],
        out_shape=[
            jax.ShapeDtypeStruct((n_tok, ATTN_WIDTH), BF16),
            jax.ShapeDtypeStruct((n_tok, KV_WIDTH), kv_dtype),
            jax.ShapeDtypeStruct((n_tok, KV_WIDTH), kv_dtype),
            jax.ShapeDtypeStruct((n_tok, pool_width), F32),
        ],
        compiler_params=_params(("arbitrary",)),
        name="projection_rope" if rope else "projection",
    )(*args)


def _rope_tables(n_tokens):
    n_rows = n_tokens // GRID_W
    inv_freq = ROPE_THETA ** (-(jnp.arange(ROPE_QUARTER, dtype=F32) / ROPE_QUARTER))
    ang_r = jnp.arange(n_rows, dtype=F32)[:, None] * inv_freq
    ang_c = jnp.arange(GRID_W, dtype=F32)[:, None] * inv_freq
    by_row = lambda a: jnp.repeat(a, GRID_W, axis=0)
    by_col = lambda a: jnp.tile(a, (n_rows, 1))
    cos_r, sin_r = by_row(jnp.cos(ang_r)), by_row(jnp.sin(ang_r))
    cos_c, sin_c = by_col(jnp.cos(ang_c)), by_col(jnp.sin(ang_c))
    cos = jnp.concatenate([cos_r, cos_r, cos_c, cos_c], axis=-1)
    sin = jnp.concatenate([-sin_r, sin_r, -sin_c, sin_c], axis=-1)
    return cos, sin


_NT = (((1,), (1,)), ((), ()))


def _attn_kernel(*refs, n_seg):
    qbound_ref, q_ref = refs[:2]
    k_refs = refs[2:2 + 2 * n_seg:2]
    v_refs = refs[3:3 + 2 * n_seg:2]
    o_ref, k_all, vt_all, s_buf, knorm = refs[2 + 2 * n_seg:]
    n_keys, tq = s_buf.shape[1:]
    chunk = min(KEY_CHUNK, n_keys)
    n_chunks = n_keys // chunk

    @pl.when(pl.program_id(2) == 0)
    def _():
        row = 0
        for k_ref, v_ref in zip(k_refs, v_refs):
            s = k_ref.shape[1]
            k_all[row:row + s, :] = k_ref[0].astype(BF16)
            vt_all[:, row:row + s] = v_ref[0].astype(F32).T.astype(BF16)
            row += s
        kk = k_all[...].astype(F32)
        knorm[0] = jnp.max(jnp.sum(kk * kk, axis=-1, keepdims=True))

    def lanes(g):
        return slice(g * HEAD_DIM, (g + 1) * HEAD_DIM)

    bounded = qbound_ref[0] * knorm[0] <= SCORE_LIMIT * SCORE_LIMIT

    @pl.when(bounded)
    def _():
        _softmax_heads(q_ref, o_ref, k_all, vt_all, s_buf, chunk, n_chunks, lanes, shift=False)

    @pl.when(jnp.logical_not(bounded))
    def _():
        _softmax_heads(q_ref, o_ref, k_all, vt_all, s_buf, chunk, n_chunks, lanes, shift=True)


def _softmax_heads(q_ref, o_ref, k_all, vt_all, s_buf, chunk, n_chunks, lanes, shift):
    tq = s_buf.shape[2]

    @pl.loop(0, q_ref.shape[1] // tq)
    def _(sub):
        q_rows = pl.ds(pl.multiple_of(sub * tq, tq), tq)
        _softmax_heads_of(q_rows, q_ref, o_ref, k_all, vt_all, s_buf, chunk, n_chunks, lanes, shift)


def _softmax_heads_of(q_rows, q_ref, o_ref, k_all, vt_all, s_buf, chunk, n_chunks, lanes, shift):
    tq = s_buf.shape[2]

    def fold(x, op):
        return op(x.reshape(chunk // SUBLANES, SUBLANES, tq), axis=0)

    def score_chunk(g, c, m8):
        rows = slice(c * chunk, (c + 1) * chunk)
        s = lax.dot_general(k_all[rows, :], q_ref[0, q_rows, lanes(g)], _NT, preferred_element_type=F32)
        s_buf[g % 2, rows, :] = s
        if not shift:
            return None
        cm = fold(s, jnp.max)
        return cm if m8 is None else jnp.maximum(m8, cm)

    def value_chunk(g, c, m, l8, acc):
        rows = slice(c * chunk, (c + 1) * chunk)
        s = s_buf[g % 2, rows, :]
        p = jnp.exp2(s - m if shift else s)
        part = jnp.dot(vt_all[:, rows], p.astype(BF16), preferred_element_type=F32)
        ps = fold(p, jnp.sum)
        return (ps, part) if l8 is None else (l8 + ps, acc + part)

    m_prev = None
    for g in range(Q_PER_KV + 1):
        m8, l8, acc = None, None, None
        for c in range(n_chunks):
            if g < Q_PER_KV:
                m8 = score_chunk(g, c, m8)
            if g > 0:
                l8, acc = value_chunk(g - 1, c, m_prev, l8, acc)
        if g > 0:
            denom = jnp.sum(l8, axis=0, keepdims=True)
            o_ref[0, q_rows, lanes(g - 1)] = (acc / denom).T.astype(o_ref.dtype)
        if g < Q_PER_KV and shift:
            m_prev = jnp.max(m8, axis=0, keepdims=True)


def _attention(q, q_bound, segments, tq):
    b, t, _ = q.shape
    group_width = Q_PER_KV * HEAD_DIM
    in_specs = [pl.BlockSpec(memory_space=pltpu.SMEM),
                pl.BlockSpec((1, tq, group_width), lambda bi, h, qi: (bi, qi, h))]
    args = [q_bound, q]
    n_keys = sum(k.shape[1] for k, _ in segments)
    assert n_keys % min(KEY_CHUNK, n_keys) == 0 and tq % min(tq, TQ_SUB) == 0
    for k, v in segments:
        s = k.shape[1]
        in_specs += [pl.BlockSpec((1, s, HEAD_DIM), lambda bi, h, qi: (bi, 0, h))] * 2
        args += [k, v]
    return pl.pallas_call(
        functools.partial(_attn_kernel, n_seg=len(segments)),
        grid=(b, N_KV_HEADS, t // tq),
        in_specs=in_specs,
        out_specs=pl.BlockSpec((1, tq, group_width), lambda bi, h, qi: (bi, qi, h)),
        out_shape=jax.ShapeDtypeStruct((b, t, ATTN_WIDTH), BF16),
        scratch_shapes=[pltpu.VMEM((n_keys, HEAD_DIM), BF16), pltpu.VMEM((HEAD_DIM, n_keys), BF16),
                        pltpu.VMEM((2, n_keys, min(tq, TQ_SUB)), F32), pltpu.SMEM((1,), F32)],
        compiler_params=_params(("arbitrary", "arbitrary", "arbitrary")),
        name="attention",
    )(*args)


def _window_sum(z, w):
    n = z.shape[0]
    span = 1
    while 2 * span < w:
        z = z + pltpu.roll(z, n - span, 0)
        span *= 2
    return z + pltpu.roll(z, w // 2, 0)


def _mix_kernel(attn_ref, pprev_ref, pcur_ref, pnext_ref, x_ref, mod_ref, wpool_ref, pscale_ref,
                wout_ref, g1_ref, b1_ref, x1_ref, u2_ref, *, tm, seq_len):
    group_width = pcur_ref.shape[1] // N_POOL_GROUPS
    block_pos = (pl.program_id(0) * tm) % seq_len
    n_half = tm // SUB
    gate1 = mod_ref[0, 2:3, :]
    shift2 = mod_ref[0, 3:4, :]
    scale2 = mod_ref[0, 4:5, :]

    for half in range(n_half):
        body = slice(half * SUB, (half + 1) * SUB)
        pos0 = (block_pos + half * SUB) % seq_len
        before = pprev_ref[...] if half == 0 else pcur_ref[half * SUB - HALO:half * SUB, :]
        after = pnext_ref[...] if half == n_half - 1 else pcur_ref[(half + 1) * SUB:(half + 1) * SUB + HALO, :]
        before = jnp.where(pos0 > 0, before, 0.0)
        after = jnp.where(pos0 + SUB < seq_len, after, 0.0)
        cur = pcur_ref[body, :]
        ext = jnp.concatenate([before, cur, after], axis=0)
        pos = pos0 + lax.broadcasted_iota(jnp.int32, (SUB, group_width), 0)
        pieces = [attn_ref[body, :]]
        for g, w in enumerate(POOL_WINDOWS):
            lanes = slice(g * group_width, (g + 1) * group_width)
            total = _window_sum(ext[:, lanes], w)[HALO:HALO + SUB]
            count = jnp.minimum(pos + w // 2, seq_len) - jnp.maximum(pos - w // 2, 0)
            pooled = total / count.astype(F32) - cur[:, lanes]
            mixed = jnp.dot(pooled.astype(BF16), wpool_ref[g], preferred_element_type=F32)
            pieces.append((mixed * pscale_ref[:, lanes]).astype(BF16))
        mix = jnp.dot(jnp.concatenate(pieces, axis=-1), wout_ref[...], preferred_element_type=F32)
        x1 = _layer_norm(DEEPNORM_ALPHA * x_ref[body, :] + gate1 * mix) * g1_ref[...] + b1_ref[...]
        x1_ref[body, :] = x1
        u2_ref[body, :] = (_layer_norm(x1) * (1.0 + scale2) + shift2).astype(BF16)


def _mix(attn2d, p2d, x2d, mod, mod_row, w_pool, pool_scale, w_out, ln_g, ln_b, seq_len, tm):
    n_tok, d = x2d.shape
    pool_width = p2d.shape[1]
    assert tm % SUB == 0 and seq_len % SUB == 0 and (seq_len % tm == 0 or tm % seq_len == 0)
    halo_per_block = tm // HALO
    n_halo_blocks = n_tok // HALO
    return pl.pallas_call(
        functools.partial(_mix_kernel, tm=tm, seq_len=seq_len),
        grid=(n_tok // tm,),
        in_specs=[
            pl.BlockSpec((tm, ATTN_WIDTH), lambda i: (i, 0)),
            pl.BlockSpec((HALO, pool_width), lambda i: (jnp.maximum(i * halo_per_block - 1, 0), 0)),
            pl.BlockSpec((tm, pool_width), lambda i: (i, 0)),
            pl.BlockSpec((HALO, pool_width),
                         lambda i: (jnp.minimum((i + 1) * halo_per_block, n_halo_blocks - 1), 0)),
            pl.BlockSpec((tm, d), lambda i: (i, 0)),
            pl.BlockSpec((1, N_MOD, d), lambda i: (mod_row(i, tm), 0, 0)),
            _resident(w_pool.shape),
            _resident((1, pool_width)),
            _resident(w_out.shape),
            _resident((1, d)),
            _resident((1, d)),
        ],
        out_specs=[
            pl.BlockSpec((tm, d), lambda i: (i, 0)),
            pl.BlockSpec((tm, d), lambda i: (i, 0)),
        ],
        out_shape=[
            jax.ShapeDtypeStruct((n_tok, d), F32),
            jax.ShapeDtypeStruct((n_tok, d), BF16),
        ],
        compiler_params=_params(("arbitrary",)),
        name="mix",
    )(attn2d, p2d, p2d, p2d, x2d, mod, w_pool, pool_scale, w_out, ln_g, ln_b)


def _mlp_kernel(u_ref, x1_ref, mod_ref, w1_ref, w2_ref, g2_ref, b2_ref, o_ref, acc_ref):
    j = pl.program_id(1)

    @pl.when(j == 0)
    def _():
        acc_ref[...] = jnp.zeros_like(acc_ref)

    h = jnp.maximum(jnp.dot(u_ref[...], w1_ref[...], preferred_element_type=F32), 0.0)
    acc_ref[...] += jnp.dot((h * h).astype(BF16), w2_ref[...], preferred_element_type=F32)

    @pl.when(j == pl.num_programs(1) - 1)
    def _():
        gate2 = mod_ref[0, 5:6, :]
        y = DEEPNORM_ALPHA * x1_ref[...] + gate2 * acc_ref[...]
        o_ref[...] = _layer_norm(y) * g2_ref[...] + b2_ref[...]


def _mlp(u2, x1, mod, mod_row, w1, w2, ln_g, ln_b, tm, tf):
    n_tok, d = x1.shape
    d_ff = w1.shape[1]
    return pl.pallas_call(
        _mlp_kernel,
        grid=(n_tok // tm, d_ff // tf),
        in_specs=[
            pl.BlockSpec((tm, d), lambda i, j: (i, 0)),
            pl.BlockSpec((tm, d), lambda i, j: (i, 0)),
            pl.BlockSpec((1, N_MOD, d), lambda i, j: (mod_row(i, tm), 0, 0)),
            pl.BlockSpec((d, tf), lambda i, j: (0, j)),
            pl.BlockSpec((tf, d), lambda i, j: (j, 0)),
            _resident((1, d)),
            _resident((1, d)),
        ],
        out_specs=pl.BlockSpec((tm, d), lambda i, j: (i, 0)),
        out_shape=jax.ShapeDtypeStruct((n_tok, d), F32),
        scratch_shapes=[pltpu.VMEM((tm, d), F32)],
        compiler_params=_params(("arbitrary", "arbitrary")),
        name="mlp",
    )(u2, x1, mod, w1, w2, ln_g, ln_b)


def _layer(x, mod, mod_row, weights, rope_tables, cache, tm_proj, tq, tm_mix, tm_mlp, tf):
    (w_in, q_gain, k_gain, w_pool, pool_scale, w_out, ln1_g, ln1_b, w_ff1, w_ff2, ln2_g, ln2_b) = weights
    b, t, d = x.shape
    x2d = x.reshape(b * t, d)
    kv_dtype = F32 if cache is None else BF16
    q, k, v, p = _projection(x2d, mod, mod_row, w_in, q_gain, k_gain, rope_tables, t, tm_proj, kv_dtype)
    k3 = k.reshape(b, t, KV_WIDTH)
    v3 = v.reshape(b, t, KV_WIDTH)
    segments = ([] if cache is None else [cache]) + [(k3, v3)]
    q_bound = (HEAD_DIM * Q_SCALE ** 2 * NORM_MARGIN * jnp.max(q_gain * q_gain)).reshape(1)
    attn = _attention(q.reshape(b, t, ATTN_WIDTH), q_bound, segments, tq)
    x1, u2 = _mix(attn.reshape(b * t, ATTN_WIDTH), p, x2d, mod, mod_row, w_pool, pool_scale, w_out,
                  ln1_g, ln1_b, t, tm_mix)
    y = _mlp(u2, x1, mod, mod_row, w_ff1, w_ff2, ln2_g, ln2_b, tm_mlp, tf)
    return y.reshape(b, t, d), k3, v3


def kernel(x_prompt, x_sample, cache_k, cache_v, c, c_ctx, w_mod, b_mod, w_in, q_gain, k_gain, w_pool,
           pool_scale, w_out, ln1_g, ln1_b, w_ff1, w_ff2, ln2_g, ln2_b):
    assert w_mod.shape[0] == DEPTH and cache_k.shape[1] == DEPTH
    n_ctx, t_ctx, d = x_prompt.shape
    n_lat, t_lat, _ = x_sample.shape
    past = cache_k.shape[2]

    n_rows = -(-(1 + n_lat) // (2 * SUBLANES)) * (2 * SUBLANES)
    cond = jnp.concatenate([c_ctx[None], c, jnp.zeros((n_rows - 1 - n_lat, d), F32)], axis=0)
    mod = _modulation(cond, w_mod[0], b_mod[0]).reshape(n_rows, N_MOD, d)

    row2 = lambda a: a[0].reshape(1, -1)
    weights = (w_in[0].astype(BF16), row2(q_gain), row2(k_gain), w_pool[0].astype(BF16), row2(pool_scale),
               w_out[0].astype(BF16), row2(ln1_g), row2(ln1_b), w_ff1[0].astype(BF16), w_ff2[0].astype(BF16),
               row2(ln2_g), row2(ln2_b))

    y_ctx, k_ctx, v_ctx = _layer(
        x_prompt, mod, lambda i, tm: 0, weights, None, None,
        tm_proj=512, tq=t_ctx, tm_mix=512, tm_mlp=512, tf=1024)

    cache = (cache_k.reshape(n_lat, past, KV_WIDTH), cache_v.reshape(n_lat, past, KV_WIDTH))
    y_lat, _, _ = _layer(
        x_sample, mod, lambda i, tm: 1 + i // (t_lat // tm), weights, _rope_tables(t_lat), cache,
        tm_proj=512, tq=1024, tm_mix=512, tm_mlp=512, tf=1024)

    ctx_k = k_ctx.reshape(n_ctx, 1, t_ctx, N_KV_HEADS, HEAD_DIM)
    ctx_v = v_ctx.reshape(n_ctx, 1, t_ctx, N_KV_HEADS, HEAD_DIM)
    return (y_ctx, y_lat, ctx_k, ctx_v)
```

```python
import functools

import jax
import jax.numpy as jnp
from jax import lax
from jax.experimental import pallas as pl
from jax.experimental.pallas import tpu as pltpu

HEAD_DIM = 128
N_Q_HEADS = 8
N_KV_HEADS = 2
Q_PER_KV = N_Q_HEADS // N_KV_HEADS
ATTN_WIDTH = N_Q_HEADS * HEAD_DIM
KV_WIDTH = N_KV_HEADS * HEAD_DIM
POOL_WINDOWS = (2, 4, 8, 16)
N_POOL_GROUPS = len(POOL_WINDOWS)
GRID_W = 64
ROPE_THETA = 10000.0
ROPE_QUARTER = HEAD_DIM // 4
EPS = 1e-6
N_MOD = 6
DEPTH = 1
DEEPNORM_ALPHA = (2.0 * DEPTH) ** 0.25
LOG2_E = 1.4426950408889634
Q_SCALE = LOG2_E * HEAD_DIM ** -0.5
NORM_MARGIN = 1.02
HALO = max(POOL_WINDOWS) // 2
SUBLANES = 8
KEY_CHUNK = 1536
SCORE_LIMIT = 64.0
SUB = 256
MXU_COLS = 256
HEADS_PER_DOT = MXU_COLS // HEAD_DIM
VMEM_LIMIT = 56 * 1024 * 1024

BF16 = jnp.bfloat16
F32 = jnp.float32


def _params(semantics):
    return pltpu.CompilerParams(dimension_semantics=semantics, vmem_limit_bytes=VMEM_LIMIT)


def _resident(shape):
    return pl.BlockSpec(shape, lambda *_: (0,) * len(shape), pipeline_mode=pl.Buffered(1))


def _layer_norm(x):
    mu = jnp.mean(x, axis=-1, keepdims=True)
    xc = x - mu
    var = jnp.mean(xc * xc, axis=-1, keepdims=True)
    return xc * lax.rsqrt(var + EPS)


def _mod_kernel(c_ref, w_ref, b_ref, o_ref):
    c = c_ref[...]
    s = c * (1.0 / (1.0 + jnp.exp(-c)))
    acc = jnp.dot(s.astype(BF16), w_ref[...].astype(BF16), preferred_element_type=F32)
    o_ref[...] = acc + b_ref[...]


def _modulation(cond, w_mod, b_mod, tn=1024):
    rows, d = cond.shape
    n = w_mod.shape[1]
    return pl.pallas_call(
        _mod_kernel,
        grid=(n // tn,),
        in_specs=[
            pl.BlockSpec((rows, d), lambda j: (0, 0)),
            pl.BlockSpec((d, tn), lambda j: (0, j)),
            pl.BlockSpec((1, tn), lambda j: (0, j)),
        ],
        out_specs=pl.BlockSpec((rows, tn), lambda j: (0, j)),
        out_shape=jax.ShapeDtypeStruct((rows, n), F32),
        compiler_params=_params(("arbitrary",)),
        name="modulation",
    )(cond, w_mod, b_mod.reshape(1, n))


def _swap_rope_halves(x):
    lane = lax.broadcasted_iota(jnp.int32, x.shape, 1)
    first = (lane % (2 * ROPE_QUARTER)) < ROPE_QUARTER
    up = pltpu.roll(x, HEAD_DIM - ROPE_QUARTER, 1)
    down = pltpu.roll(x, ROPE_QUARTER, 1)
    return jnp.where(first, up, down)


def _proj_kernel(*refs, rope, q_scale):
    if rope:
        x_ref, mod_ref, w_ref, qg_ref, kg_ref, cos_ref, sin_ref, q_ref, k_ref, v_ref, p_ref = refs
    else:
        x_ref, mod_ref, w_ref, qg_ref, kg_ref, q_ref, k_ref, v_ref, p_ref = refs
    shift = mod_ref[0, 0:1, :]
    scale = mod_ref[0, 1:2, :]
    v0 = ATTN_WIDTH + KV_WIDTH
    p0 = v0 + KV_WIDTH

    for half in range(x_ref.shape[0] // SUB):
        body = slice(half * SUB, (half + 1) * SUB)
        u = (_layer_norm(x_ref[body, :]) * (1.0 + scale) + shift).astype(BF16)

        def heads(out_ref, col0, n_heads, gain_ref, post_scale):
            for first in range(0, n_heads, HEADS_PER_DOT):
                c0 = col0 + first * HEAD_DIM
                raw_all = jnp.dot(u, w_ref[:, c0:c0 + MXU_COLS], preferred_element_type=F32)
                for i in range(HEADS_PER_DOT):
                    raw = raw_all[:, i * HEAD_DIM:(i + 1) * HEAD_DIM]
                    ms = jnp.mean(raw * raw, axis=-1, keepdims=True)
                    y = raw * lax.rsqrt(ms + EPS) * gain_ref[...]
                    if rope:
                        y = y * cos_ref[body, :] + _swap_rope_halves(y) * sin_ref[body, :]
                    if post_scale != 1.0:
                        y = y * post_scale
                    out = (first + i) * HEAD_DIM
                    out_ref[body, out:out + HEAD_DIM] = y.astype(out_ref.dtype)

        heads(q_ref, 0, N_Q_HEADS, qg_ref, q_scale)
        heads(k_ref, ATTN_WIDTH, N_KV_HEADS, kg_ref, 1.0)
        v_ref[body, :] = jnp.dot(u, w_ref[:, v0:v0 + KV_WIDTH], preferred_element_type=F32).astype(v_ref.dtype)
        p_ref[body, :] = jnp.dot(u, w_ref[:, p0:], preferred_element_type=F32).astype(p_ref.dtype)


def _projection(x2d, mod, mod_row, w_in, q_gain, k_gain, rope_tables, seq_len, tm, kv_dtype):
    n_tok, d = x2d.shape
    in_width = w_in.shape[1]
    pool_width = in_width - ATTN_WIDTH - 2 * KV_WIDTH
    rope = rope_tables is not None
    assert tm % SUB == 0
    seq_blocks = seq_len // tm if rope else None
    in_specs = [
        pl.BlockSpec((tm, d), lambda i: (i, 0)),
        pl.BlockSpec((1, N_MOD, d), lambda i: (mod_row(i, tm), 0, 0)),
        _resident((d, in_width)),
        _resident((1, HEAD_DIM)),
        _resident((1, HEAD_DIM)),
    ]
    args = [x2d, mod, w_in, q_gain, k_gain]
    if rope:
        in_specs += [pl.BlockSpec((tm, HEAD_DIM), lambda i: (i % seq_blocks, 0))] * 2
        args += list(rope_tables)
    return pl.pallas_call(
        functools.partial(_proj_kernel, rope=rope, q_scale=Q_SCALE),
        grid=(n_tok // tm,),
        in_specs=in_specs,
        out_specs=[
            pl.BlockSpec((tm, ATTN_WIDTH), lambda i: (i, 0)),
            pl.BlockSpec((tm, KV_WIDTH), lambda i: (i, 0)),
            pl.BlockSpec((tm, KV_WIDTH), lambda i: (i, 0)),
            pl.BlockSpec((tm, pool_width), lambda i: (i, 0)),
        ],
        out_shape=[
            jax.ShapeDtypeStruct((n_tok, ATTN_WIDTH), BF16),
            jax.ShapeDtypeStruct((n_tok, KV_WIDTH), kv_dtype),
            jax.ShapeDtypeStruct((n_tok, KV_WIDTH), kv_dtype),
            jax.ShapeDtypeStruct((n_tok, pool_width), F32),
        ],
        compiler_params=_params(("arbitrary",)),
        name="projection_rope" if rope else "projection",
    )(*args)


def _rope_tables(n_tokens):
    n_rows = n_tokens // GRID_W
    inv_freq = ROPE_THETA ** (-(jnp.arange(ROPE_QUARTER, dtype=F32) / ROPE_QUARTER))
    ang_r = jnp.arange(n_rows, dtype=F32)[:, None] * inv_freq
    ang_c = jnp.arange(GRID_W, dtype=F32)[:, None] * inv_freq
    by_row = lambda a: jnp.repeat(a, GRID_W, axis=0)
    by_col = lambda a: jnp.tile(a, (n_rows, 1))
    cos_r, sin_r = by_row(jnp.cos(ang_r)), by_row(jnp.sin(ang_r))
    cos_c, sin_c = by_col(jnp.cos(ang_c)), by_col(jnp.sin(ang_c))
    cos = jnp.concatenate([cos_r, cos_r, cos_c, cos_c], axis=-1)
    sin = jnp.concatenate([-sin_r, sin_r, -sin_c, sin_c], axis=-1)
    return cos, sin


_NT = (((1,), (1,)), ((), ()))


def _attn_kernel(*refs, n_seg):
    qbound_ref, q_ref = refs[:2]
    k_refs = refs[2:2 + 2 * n_seg:2]
    v_refs = refs[3:3 + 2 * n_seg:2]
    o_ref, k_all, vt_all, s_buf, p_buf, knorm = refs[2 + 2 * n_seg:]
    n_keys, tq = s_buf.shape[1:]
    chunk = min(KEY_CHUNK, n_keys)
    n_chunks = n_keys // chunk

    @pl.when(pl.program_id(2) == 0)
    def _():
        row = 0
        for k_ref, v_ref in zip(k_refs, v_refs):
            s = k_ref.shape[1]
            k_all[row:row + s, :] = k_ref[0].astype(BF16)
            vt_all[:, row:row + s] = v_ref[0].astype(F32).T.astype(BF16)
            row += s
        kk = k_all[...].astype(F32)
        knorm[0] = jnp.max(jnp.sum(kk * kk, axis=-1, keepdims=True))

    def lanes(g):
        return slice(g * HEAD_DIM, (g + 1) * HEAD_DIM)

    bounded = qbound_ref[0] * knorm[0] <= SCORE_LIMIT * SCORE_LIMIT

    @pl.when(bounded)
    def _():
        _unshifted_softmax_heads(q_ref, o_ref, k_all, vt_all, p_buf, chunk, n_chunks, lanes)

    @pl.when(jnp.logical_not(bounded))
    def _():
        _softmax_heads(q_ref, o_ref, k_all, vt_all, s_buf, chunk, n_chunks, lanes, shift=True)


def _softmax_heads(q_ref, o_ref, k_all, vt_all, s_buf, chunk, n_chunks, lanes, shift):
    tq = s_buf.shape[2]

    def fold(x, op):
        return op(x.reshape(chunk // SUBLANES, SUBLANES, tq), axis=0)

    def score_chunk(g, c, m8):
        rows = slice(c * chunk, (c + 1) * chunk)
        s = lax.dot_general(k_all[rows, :], q_ref[0, :, lanes(g)], _NT, preferred_element_type=F32)
        s_buf[g % 2, rows, :] = s
        if not shift:
            return None
        cm = fold(s, jnp.max)
        return cm if m8 is None else jnp.maximum(m8, cm)

    def value_chunk(g, c, m, l8, acc):
        rows = slice(c * chunk, (c + 1) * chunk)
        s = s_buf[g % 2, rows, :]
        p = jnp.exp2(s - m if shift else s)
        part = jnp.dot(vt_all[:, rows], p.astype(BF16), preferred_element_type=F32)
        ps = fold(p, jnp.sum)
        return (ps, part) if l8 is None else (l8 + ps, acc + part)

    m_prev = None
    for g in range(Q_PER_KV + 1):
        m8, l8, acc = None, None, None
        for c in range(n_chunks):
            if g < Q_PER_KV:
                m8 = score_chunk(g, c, m8)
            if g > 0:
                l8, acc = value_chunk(g - 1, c, m_prev, l8, acc)
        if g > 0:
            denom = jnp.sum(l8, axis=0, keepdims=True)
            o_ref[0, :, lanes(g - 1)] = (acc / denom).T.astype(o_ref.dtype)
        if g < Q_PER_KV and shift:
            m_prev = jnp.max(m8, axis=0, keepdims=True)


def _unshifted_softmax_heads(q_ref, o_ref, k_all, vt_all, p_buf, chunk, n_chunks, lanes):
    tq = p_buf.shape[2]

    l_prev = None
    for g in range(Q_PER_KV + 1):
        l8, acc = None, None
        for c in range(n_chunks):
            rows = slice(c * chunk, (c + 1) * chunk)
            if g < Q_PER_KV:
                s = lax.dot_general(k_all[rows, :], q_ref[0, :, lanes(g)], _NT, preferred_element_type=F32)
                p = jnp.exp2(s)
                p_buf[g % 2, rows, :] = p.astype(BF16)
                ps = jnp.sum(p.reshape(chunk // SUBLANES, SUBLANES, tq), axis=0)
                l8 = ps if l8 is None else l8 + ps
            if g > 0:
                part = jnp.dot(vt_all[:, rows], p_buf[(g - 1) % 2, rows, :], preferred_element_type=F32)
                acc = part if acc is None else acc + part
        if g > 0:
            denom = jnp.sum(l_prev, axis=0, keepdims=True)
            o_ref[0, :, lanes(g - 1)] = (acc / denom).T.astype(o_ref.dtype)
        l_prev = l8


def _attention(q, q_bound, segments, tq):
    b, t, _ = q.shape
    group_width = Q_PER_KV * HEAD_DIM
    in_specs = [pl.BlockSpec(memory_space=pltpu.SMEM),
                pl.BlockSpec((1, tq, group_width), lambda bi, h, qi: (bi, qi, h))]
    args = [q_bound, q]
    n_keys = sum(k.shape[1] for k, _ in segments)
    assert n_keys % min(KEY_CHUNK, n_keys) == 0
    for k, v in segments:
        s = k.shape[1]
        in_specs += [pl.BlockSpec((1, s, HEAD_DIM), lambda bi, h, qi: (bi, 0, h))] * 2
        args += [k, v]
    return pl.pallas_call(
        functools.partial(_attn_kernel, n_seg=len(segments)),
        grid=(b, N_KV_HEADS, t // tq),
        in_specs=in_specs,
        out_specs=pl.BlockSpec((1, tq, group_width), lambda bi, h, qi: (bi, qi, h)),
        out_shape=jax.ShapeDtypeStruct((b, t, ATTN_WIDTH), BF16),
        scratch_shapes=[pltpu.VMEM((n_keys, HEAD_DIM), BF16), pltpu.VMEM((HEAD_DIM, n_keys), BF16),
                        pltpu.VMEM((2, n_keys, tq), F32), pltpu.VMEM((2, n_keys, tq), BF16),
                        pltpu.SMEM((1,), F32)],
        compiler_params=_params(("arbitrary", "arbitrary", "arbitrary")),
        name="attention",
    )(*args)


def _window_sum(z, w):
    n = z.shape[0]
    span = 1
    while 2 * span < w:
        z = z + pltpu.roll(z, n - span, 0)
        span *= 2
    return z + pltpu.roll(z, w // 2, 0)


def _mix_kernel(attn_ref, pprev_ref, pcur_ref, pnext_ref, x_ref, mod_ref, wpool_ref, pscale_ref,
                wout_ref, g1_ref, b1_ref, x1_ref, u2_ref, *, tm, seq_len):
    group_width = pcur_ref.shape[1] // N_POOL_GROUPS
    block_pos = (pl.program_id(0) * tm) % seq_len
    n_half = tm // SUB
    gate1 = mod_ref[0, 2:3, :]
    shift2 = mod_ref[0, 3:4, :]
    scale2 = mod_ref[0, 4:5, :]

    for half in range(n_half):
        body = slice(half * SUB, (half + 1) * SUB)
        pos0 = (block_pos + half * SUB) % seq_len
        before = pprev_ref[...] if half == 0 else pcur_ref[half * SUB - HALO:half * SUB, :]
        after = pnext_ref[...] if half == n_half - 1 else pcur_ref[(half + 1) * SUB:(half + 1) * SUB + HALO, :]
        before = jnp.where(pos0 > 0, before, 0.0)
        after = jnp.where(pos0 + SUB < seq_len, after, 0.0)
        cur = pcur_ref[body, :]
        ext = jnp.concatenate([before, cur, after], axis=0)
        pos = pos0 + lax.broadcasted_iota(jnp.int32, (SUB, group_width), 0)
        pieces = [attn_ref[body, :]]
        for g, w in enumerate(POOL_WINDOWS):
            lanes = slice(g * group_width, (g + 1) * group_width)
            total = _window_sum(ext[:, lanes], w)[HALO:HALO + SUB]
            count = jnp.minimum(pos + w // 2, seq_len) - jnp.maximum(pos - w // 2, 0)
            pooled = total / count.astype(F32) - cur[:, lanes]
            mixed = jnp.dot(pooled.astype(BF16), wpool_ref[g], preferred_element_type=F32)
            pieces.append((mixed * pscale_ref[:, lanes]).astype(BF16))
        mix = jnp.dot(jnp.concatenate(pieces, axis=-1), wout_ref[...], preferred_element_type=F32)
        x1 = _layer_norm(DEEPNORM_ALPHA * x_ref[body, :] + gate1 * mix) * g1_ref[...] + b1_ref[...]
        x1_ref[body, :] = x1
        u2_ref[body, :] = (_layer_norm(x1) * (1.0 + scale2) + shift2).astype(BF16)


def _mix(attn2d, p2d, x2d, mod, mod_row, w_pool, pool_scale, w_out, ln_g, ln_b, seq_len, tm):
    n_tok, d = x2d.shape
    pool_width = p2d.shape[1]
    assert tm % SUB == 0 and seq_len % SUB == 0 and (seq_len % tm == 0 or tm % seq_len == 0)
    halo_per_block = tm // HALO
    n_halo_blocks = n_tok // HALO
    return pl.pallas_call(
        functools.partial(_mix_kernel, tm=tm, seq_len=seq_len),
        grid=(n_tok // tm,),
        in_specs=[
            pl.BlockSpec((tm, ATTN_WIDTH), lambda i: (i, 0)),
            pl.BlockSpec((HALO, pool_width), lambda i: (jnp.maximum(i * halo_per_block - 1, 0), 0)),
            pl.BlockSpec((tm, pool_width), lambda i: (i, 0)),
            pl.BlockSpec((HALO, pool_width),
                         lambda i: (jnp.minimum((i + 1) * halo_per_block, n_halo_blocks - 1), 0)),
            pl.BlockSpec((tm, d), lambda i: (i, 0)),
            pl.BlockSpec((1, N_MOD, d), lambda i: (mod_row(i, tm), 0, 0)),
            _resident(w_pool.shape),
            _resident((1, pool_width)),
            _resident(w_out.shape),
            _resident((1, d)),
            _resident((1, d)),
        ],
        out_specs=[
            pl.BlockSpec((tm, d), lambda i: (i, 0)),
            pl.BlockSpec((tm, d), lambda i: (i, 0)),
        ],
        out_shape=[
            jax.ShapeDtypeStruct((n_tok, d), F32),
            jax.ShapeDtypeStruct((n_tok, d), BF16),
        ],
        compiler_params=_params(("arbitrary",)),
        name="mix",
    )(attn2d, p2d, p2d, p2d, x2d, mod, w_pool, pool_scale, w_out, ln_g, ln_b)


def _mlp_kernel(u_ref, x1_ref, mod_ref, w1_ref, w2_ref, g2_ref, b2_ref, o_ref, acc_ref):
    j = pl.program_id(1)

    @pl.when(j == 0)
    def _():
        acc_ref[...] = jnp.zeros_like(acc_ref)

    h = jnp.maximum(jnp.dot(u_ref[...], w1_ref[...], preferred_element_type=F32), 0.0)
    acc_ref[...] += jnp.dot((h * h).astype(BF16), w2_ref[...], preferred_element_type=F32)

    @pl.when(j == pl.num_programs(1) - 1)
    def _():
        gate2 = mod_ref[0, 5:6, :]
        y = DEEPNORM_ALPHA * x1_ref[...] + gate2 * acc_ref[...]
        o_ref[...] = _layer_norm(y) * g2_ref[...] + b2_ref[...]


def _mlp(u2, x1, mod, mod_row, w1, w2, ln_g, ln_b, tm, tf):
    n_tok, d = x1.shape
    d_ff = w1.shape[1]
    return pl.pallas_call(
        _mlp_kernel,
        grid=(n_tok // tm, d_ff // tf),
        in_specs=[
            pl.BlockSpec((tm, d), lambda i, j: (i, 0)),
            pl.BlockSpec((tm, d), lambda i, j: (i, 0)),
            pl.BlockSpec((1, N_MOD, d), lambda i, j: (mod_row(i, tm), 0, 0)),
            pl.BlockSpec((d, tf), lambda i, j: (0, j)),
            pl.BlockSpec((tf, d), lambda i, j: (j, 0)),
            _resident((1, d)),
            _resident((1, d)),
        ],
        out_specs=pl.BlockSpec((tm, d), lambda i, j: (i, 0)),
        out_shape=jax.ShapeDtypeStruct((n_tok, d), F32),
        scratch_shapes=[pltpu.VMEM((tm, d), F32)],
        compiler_params=_params(("arbitrary", "arbitrary")),
        name="mlp",
    )(u2, x1, mod, w1, w2, ln_g, ln_b)


def _layer(x, mod, mod_row, weights, rope_tables, cache, tm_proj, tq, tm_mix, tm_mlp, tf):
    (w_in, q_gain, k_gain, w_pool, pool_scale, w_out, ln1_g, ln1_b, w_ff1, w_ff2, ln2_g, ln2_b) = weights
    b, t, d = x.shape
    x2d = x.reshape(b * t, d)
    kv_dtype = F32 if cache is None else BF16
    q, k, v, p = _projection(x2d, mod, mod_row, w_in, q_gain, k_gain, rope_tables, t, tm_proj, kv_dtype)
    k3 = k.reshape(b, t, KV_WIDTH)
    v3 = v.reshape(b, t, KV_WIDTH)
    segments = ([] if cache is None else [cache]) + [(k3, v3)]
    q_bound = (HEAD_DIM * Q_SCALE ** 2 * NORM_MARGIN * jnp.max(q_gain * q_gain)).reshape(1)
    attn = _attention(q.reshape(b, t, ATTN_WIDTH), q_bound, segments, tq)
    x1, u2 = _mix(attn.reshape(b * t, ATTN_WIDTH), p, x2d, mod, mod_row, w_pool, pool_scale, w_out,
                  ln1_g, ln1_b, t, tm_mix)
    y = _mlp(u2, x1, mod, mod_row, w_ff1, w_ff2, ln2_g, ln2_b, tm_mlp, tf)
    return y.reshape(b, t, d), k3, v3


def kernel(x_prompt, x_sample, cache_k, cache_v, c, c_ctx, w_mod, b_mod, w_in, q_gain, k_gain, w_pool,
           pool_scale, w_out, ln1_g, ln1_b, w_ff1, w_ff2, ln2_g, ln2_b):
    assert w_mod.shape[0] == DEPTH and cache_k.shape[1] == DEPTH
    n_ctx, t_ctx, d = x_prompt.shape
    n_lat, t_lat, _ = x_sample.shape
    past = cache_k.shape[2]

    n_rows = -(-(1 + n_lat) // (2 * SUBLANES)) * (2 * SUBLANES)
    cond = jnp.concatenate([c_ctx[None], c, jnp.zeros((n_rows - 1 - n_lat, d), F32)], axis=0)
    mod = _modulation(cond, w_mod[0], b_mod[0]).reshape(n_rows, N_MOD, d)

    row2 = lambda a: a[0].reshape(1, -1)
    weights = (w_in[0].astype(BF16), row2(q_gain), row2(k_gain), w_pool[0].astype(BF16), row2(pool_scale),
               w_out[0].astype(BF16), row2(ln1_g), row2(ln1_b), w_ff1[0].astype(BF16), w_ff2[0].astype(BF16),
               row2(ln2_g), row2(ln2_b))

    y_ctx, k_ctx, v_ctx = _layer(
        x_prompt, mod, lambda i, tm: 0, weights, None, None,
        tm_proj=512, tq=t_ctx, tm_mix=512, tm_mlp=512, tf=1024)

    cache = (cache_k.reshape(n_lat, past, KV_WIDTH), cache_v.reshape(n_lat, past, KV_WIDTH))
    y_lat, _, _ = _layer(
        x_sample, mod, lambda i, tm: 1 + i // (t_lat // tm), weights, _rope_tables(t_lat), cache,
        tm_proj=512, tq=512, tm_mix=512, tm_mlp=512, tf=1024)

    ctx_k = k_ctx.reshape(n_ctx, 1, t_ctx, N_KV_HEADS, HEAD_DIM)
    ctx_v = v_ctx.reshape(n_ctx, 1, t_ctx, N_KV_HEADS, HEAD_DIM)
    return (y_ctx, y_lat, ctx_k, ctx_v)
```

```python
import functools

import jax
import jax.numpy as jnp
from jax import lax
from jax.experimental import pallas as pl
from jax.experimental.pallas import tpu as pltpu

HEAD_DIM = 128
N_Q_HEADS = 8
N_KV_HEADS = 2
Q_PER_KV = N_Q_HEADS // N_KV_HEADS
ATTN_WIDTH = N_Q_HEADS * HEAD_DIM
KV_WIDTH = N_KV_HEADS * HEAD_DIM
POOL_WINDOWS = (2, 4, 8, 16)
N_POOL_GROUPS = len(POOL_WINDOWS)
GRID_W = 64
ROPE_THETA = 10000.0
ROPE_QUARTER = HEAD_DIM // 4
EPS = 1e-6
N_MOD = 6
DEPTH = 1
DEEPNORM_ALPHA = (2.0 * DEPTH) ** 0.25
LOG2_E = 1.4426950408889634
Q_SCALE = LOG2_E * HEAD_DIM ** -0.5
NORM_MARGIN = 1.02
HALO = max(POOL_WINDOWS) // 2
SUBLANES = 8
KEY_CHUNK = 1536
SCORE_LIMIT = 64.0
SUB = 256
MXU_COLS = 256
HEADS_PER_DOT = MXU_COLS // HEAD_DIM
VMEM_LIMIT = 56 * 1024 * 1024

BF16 = jnp.bfloat16
F32 = jnp.float32


def _params(semantics):
    return pltpu.CompilerParams(dimension_semantics=semantics, vmem_limit_bytes=VMEM_LIMIT)


def _resident(shape):
    return pl.BlockSpec(shape, lambda *_: (0,) * len(shape), pipeline_mode=pl.Buffered(1))


def _layer_norm(x, eps=EPS):
    mu = jnp.mean(x, axis=-1, keepdims=True)
    xc = x - mu
    var = jnp.mean(xc * xc, axis=-1, keepdims=True)
    return xc * lax.rsqrt(var + eps)


def _deepnorm(x, gate, branch):
    return _layer_norm(x + (gate * (1.0 / DEEPNORM_ALPHA)) * branch, EPS / DEEPNORM_ALPHA ** 2)


def _mod_kernel(c_ref, w_ref, b_ref, o_ref):
    c = c_ref[...]
    s = c * (1.0 / (1.0 + jnp.exp(-c)))
    acc = jnp.dot(s.astype(BF16), w_ref[...].astype(BF16), preferred_element_type=F32)
    o_ref[...] = acc + b_ref[...]


def _modulation(cond, w_mod, b_mod, tn=1024):
    rows, d = cond.shape
    n = w_mod.shape[1]
    return pl.pallas_call(
        _mod_kernel,
        grid=(n // tn,),
        in_specs=[
            pl.BlockSpec((rows, d), lambda j: (0, 0)),
            pl.BlockSpec((d, tn), lambda j: (0, j)),
            pl.BlockSpec((1, tn), lambda j: (0, j)),
        ],
        out_specs=pl.BlockSpec((rows, tn), lambda j: (0, j)),
        out_shape=jax.ShapeDtypeStruct((rows, n), F32),
        compiler_params=_params(("arbitrary",)),
        name="modulation",
    )(cond, w_mod, b_mod.reshape(1, n))


def _swap_rope_halves(x):
    lane = lax.broadcasted_iota(jnp.int32, x.shape, 1)
    first = (lane % (2 * ROPE_QUARTER)) < ROPE_QUARTER
    up = pltpu.roll(x, HEAD_DIM - ROPE_QUARTER, 1)
    down = pltpu.roll(x, ROPE_QUARTER, 1)
    return jnp.where(first, up, down)


def _proj_kernel(*refs, rope, q_scale):
    if rope:
        x_ref, mod_ref, w_ref, qg_ref, kg_ref, cos_ref, sin_ref, q_ref, k_ref, v_ref, p_ref = refs
    else:
        x_ref, mod_ref, w_ref, qg_ref, kg_ref, q_ref, k_ref, v_ref, p_ref = refs
    shift = mod_ref[0, 0:1, :]
    scale = mod_ref[0, 1:2, :]
    v0 = ATTN_WIDTH + KV_WIDTH
    p0 = v0 + KV_WIDTH

    for half in range(x_ref.shape[0] // SUB):
        body = slice(half * SUB, (half + 1) * SUB)
        u = (_layer_norm(x_ref[body, :]) * (1.0 + scale) + shift).astype(BF16)

        def heads(out_ref, col0, n_heads, gain_ref, post_scale):
            for first in range(0, n_heads, HEADS_PER_DOT):
                c0 = col0 + first * HEAD_DIM
                raw_all = jnp.dot(u, w_ref[:, c0:c0 + MXU_COLS], preferred_element_type=F32)
                for i in range(HEADS_PER_DOT):
                    raw = raw_all[:, i * HEAD_DIM:(i + 1) * HEAD_DIM]
                    ms = jnp.mean(raw * raw, axis=-1, keepdims=True)
                    y = raw * lax.rsqrt(ms + EPS) * gain_ref[...]
                    if rope:
                        y = y * cos_ref[body, :] + _swap_rope_halves(y) * sin_ref[body, :]
                    if post_scale != 1.0:
                        y = y * post_scale
                    out = (first + i) * HEAD_DIM
                    out_ref[body, out:out + HEAD_DIM] = y.astype(out_ref.dtype)

        heads(q_ref, 0, N_Q_HEADS, qg_ref, q_scale)
        heads(k_ref, ATTN_WIDTH, N_KV_HEADS, kg_ref, 1.0)
        v_ref[body, :] = jnp.dot(u, w_ref[:, v0:v0 + KV_WIDTH], preferred_element_type=F32).astype(v_ref.dtype)
        p_ref[body, :] = jnp.dot(u, w_ref[:, p0:], preferred_element_type=F32).astype(p_ref.dtype)


def _projection(x2d, mod, mod_row, w_in, q_gain, k_gain, rope_tables, seq_len, tm, kv_dtype):
    n_tok, d = x2d.shape
    in_width = w_in.shape[1]
    pool_width = in_width - ATTN_WIDTH - 2 * KV_WIDTH
    rope = rope_tables is not None
    assert tm % SUB == 0
    seq_blocks = seq_len // tm if rope else None
    in_specs = [
        pl.BlockSpec((tm, d), lambda i: (i, 0)),
        pl.BlockSpec((1, N_MOD, d), lambda i: (mod_row(i, tm), 0, 0)),
        _resident((d, in_width)),
        _resident((1, HEAD_DIM)),
        _resident((1, HEAD_DIM)),
    ]
    args = [x2d, mod, w_in, q_gain, k_gain]
    if rope:
        in_specs += [pl.BlockSpec((tm, HEAD_DIM), lambda i: (i % seq_blocks, 0))] * 2
        args += list(rope_tables)
    return pl.pallas_call(
        functools.partial(_proj_kernel, rope=rope, q_scale=Q_SCALE),
        grid=(n_tok // tm,),
        in_specs=in_specs,
        out_specs=[
            pl.BlockSpec((tm, ATTN_WIDTH), lambda i: (i, 0)),
            pl.BlockSpec((tm, KV_WIDTH), lambda i: (i, 0)),
            pl.BlockSpec((tm, KV_WIDTH), lambda i: (i, 0)),
            pl.BlockSpec((tm, pool_width), lambda i: (i, 0)),
        ],
        out_shape=[
            jax.ShapeDtypeStruct((n_tok, ATTN_WIDTH), BF16),
            jax.ShapeDtypeStruct((n_tok, KV_WIDTH), kv_dtype),
            jax.ShapeDtypeStruct((n_tok, KV_WIDTH), kv_dtype),
            jax.ShapeDtypeStruct((n_tok, pool_width), F32),
        ],
        compiler_params=_params(("arbitrary",)),
        name="projection_rope" if rope else "projection",
    )(*args)


def _rope_tables(n_tokens):
    n_rows = n_tokens // GRID_W
    inv_freq = ROPE_THETA ** (-(jnp.arange(ROPE_QUARTER, dtype=F32) / ROPE_QUARTER))
    ang_r = jnp.arange(n_rows, dtype=F32)[:, None] * inv_freq
    ang_c = jnp.arange(GRID_W, dtype=F32)[:, None] * inv_freq
    by_row = lambda a: jnp.repeat(a, GRID_W, axis=0)
    by_col = lambda a: jnp.tile(a, (n_rows, 1))
    cos_r, sin_r = by_row(jnp.cos(ang_r)), by_row(jnp.sin(ang_r))
    cos_c, sin_c = by_col(jnp.cos(ang_c)), by_col(jnp.sin(ang_c))
    cos = jnp.concatenate([cos_r, cos_r, cos_c, cos_c], axis=-1)
    sin = jnp.concatenate([-sin_r, sin_r, -sin_c, sin_c], axis=-1)
    return cos, sin


_NT = (((1,), (1,)), ((), ()))


def _attn_kernel(*refs, n_seg):
    qbound_ref, q_ref = refs[:2]
    k_refs = refs[2:2 + 2 * n_seg:2]
    v_refs = refs[3:3 + 2 * n_seg:2]
    o_ref, k_all, vt_all, s_buf, p_buf, knorm = refs[2 + 2 * n_seg:]
    n_keys, tq = s_buf.shape[1:]
    chunk = min(KEY_CHUNK, n_keys)
    n_chunks = n_keys // chunk

    @pl.when(pl.program_id(2) == 0)
    def _():
        row = 0
        for k_ref, v_ref in zip(k_refs, v_refs):
            s = k_ref.shape[1]
            k_all[row:row + s, :] = k_ref[0].astype(BF16)
            vt_all[:, row:row + s] = v_ref[0].astype(F32).T.astype(BF16)
            row += s
        kk = k_all[...].astype(F32)
        knorm[0] = jnp.max(jnp.sum(kk * kk, axis=-1, keepdims=True))

    def lanes(g):
        return slice(g * HEAD_DIM, (g + 1) * HEAD_DIM)

    bounded = qbound_ref[0] * knorm[0] <= SCORE_LIMIT * SCORE_LIMIT

    @pl.when(bounded)
    def _():
        _unshifted_softmax_heads(q_ref, o_ref, k_all, vt_all, p_buf, chunk, n_chunks, lanes)

    @pl.when(jnp.logical_not(bounded))
    def _():
        _softmax_heads(q_ref, o_ref, k_all, vt_all, s_buf, chunk, n_chunks, lanes, shift=True)


def _softmax_heads(q_ref, o_ref, k_all, vt_all, s_buf, chunk, n_chunks, lanes, shift):
    tq = s_buf.shape[2]

    def fold(x, op):
        return op(x.reshape(chunk // SUBLANES, SUBLANES, tq), axis=0)

    def score_chunk(g, c, m8):
        rows = slice(c * chunk, (c + 1) * chunk)
        s = lax.dot_general(k_all[rows, :], q_ref[0, :, lanes(g)], _NT, preferred_element_type=F32)
        s_buf[g % 2, rows, :] = s
        if not shift:
            return None
        cm = fold(s, jnp.max)
        return cm if m8 is None else jnp.maximum(m8, cm)

    def value_chunk(g, c, m, l8, acc):
        rows = slice(c * chunk, (c + 1) * chunk)
        s = s_buf[g % 2, rows, :]
        p = jnp.exp2(s - m if shift else s)
        part = jnp.dot(vt_all[:, rows], p.astype(BF16), preferred_element_type=F32)
        ps = fold(p, jnp.sum)
        return (ps, part) if l8 is None else (l8 + ps, acc + part)

    m_prev = None
    for g in range(Q_PER_KV + 1):
        m8, l8, acc = None, None, None
        for c in range(n_chunks):
            if g < Q_PER_KV:
                m8 = score_chunk(g, c, m8)
            if g > 0:
                l8, acc = value_chunk(g - 1, c, m_prev, l8, acc)
        if g > 0:
            denom = jnp.sum(l8, axis=0, keepdims=True)
            o_ref[0, :, lanes(g - 1)] = (acc / denom).T.astype(o_ref.dtype)
        if g < Q_PER_KV and shift:
            m_prev = jnp.max(m8, axis=0, keepdims=True)


def _unshifted_softmax_heads(q_ref, o_ref, k_all, vt_all, p_buf, chunk, n_chunks, lanes):
    tq = p_buf.shape[2]

    l_prev = None
    for g in range(Q_PER_KV + 1):
        l8, acc = None, None
        for c in range(n_chunks):
            rows = slice(c * chunk, (c + 1) * chunk)
            if g < Q_PER_KV:
                s = lax.dot_general(k_all[rows, :], q_ref[0, :, lanes(g)], _NT, preferred_element_type=F32)
                p = jnp.exp2(s)
                p_buf[g % 2, rows, :] = p.astype(BF16)
                ps = jnp.sum(p.reshape(chunk // SUBLANES, SUBLANES, tq), axis=0)
                l8 = ps if l8 is None else l8 + ps
            if g > 0:
                part = jnp.dot(vt_all[:, rows], p_buf[(g - 1) % 2, rows, :], preferred_element_type=F32)
                acc = part if acc is None else acc + part
        if g > 0:
            denom = jnp.sum(l_prev, axis=0, keepdims=True)
            o_ref[0, :, lanes(g - 1)] = (acc / denom).T.astype(o_ref.dtype)
        l_prev = l8


def _attention(q, q_bound, segments, tq):
    b, t, _ = q.shape
    group_width = Q_PER_KV * HEAD_DIM
    in_specs = [pl.BlockSpec(memory_space=pltpu.SMEM),
                pl.BlockSpec((1, tq, group_width), lambda bi, h, qi: (bi, qi, h))]
    args = [q_bound, q]
    n_keys = sum(k.shape[1] for k, _ in segments)
    assert n_keys % min(KEY_CHUNK, n_keys) == 0
    for k, v in segments:
        s = k.shape[1]
        in_specs += [pl.BlockSpec((1, s, HEAD_DIM), lambda bi, h, qi: (bi, 0, h))] * 2
        args += [k, v]
    return pl.pallas_call(
        functools.partial(_attn_kernel, n_seg=len(segments)),
        grid=(b, N_KV_HEADS, t // tq),
        in_specs=in_specs,
        out_specs=pl.BlockSpec((1, tq, group_width), lambda bi, h, qi: (bi, qi, h)),
        out_shape=jax.ShapeDtypeStruct((b, t, ATTN_WIDTH), BF16),
        scratch_shapes=[pltpu.VMEM((n_keys, HEAD_DIM), BF16), pltpu.VMEM((HEAD_DIM, n_keys), BF16),
                        pltpu.VMEM((2, n_keys, tq), F32), pltpu.VMEM((2, n_keys, tq), BF16),
                        pltpu.SMEM((1,), F32)],
        compiler_params=_params(("arbitrary", "arbitrary", "arbitrary")),
        name="attention",
    )(*args)


def _window_sum(z, w):
    n = z.shape[0]
    span = 1
    while 2 * span < w:
        z = z + pltpu.roll(z, n - span, 0)
        span *= 2
    return z + pltpu.roll(z, w // 2, 0)


def _mix_kernel(attn_ref, pprev_ref, pcur_ref, pnext_ref, x_ref, mod_ref, wpool_ref, pscale_ref,
                wout_ref, g1_ref, b1_ref, x1_ref, u2_ref, *, tm, seq_len):
    group_width = pcur_ref.shape[1] // N_POOL_GROUPS
    block_pos = (pl.program_id(0) * tm) % seq_len
    n_half = tm // SUB
    gate1 = mod_ref[0, 2:3, :]
    shift2 = mod_ref[0, 3:4, :]
    scale2 = mod_ref[0, 4:5, :]

    for half in range(n_half):
        body = slice(half * SUB, (half + 1) * SUB)
        pos0 = (block_pos + half * SUB) % seq_len
        before = pprev_ref[...] if half == 0 else pcur_ref[half * SUB - HALO:half * SUB, :]
        after = pnext_ref[...] if half == n_half - 1 else pcur_ref[(half + 1) * SUB:(half + 1) * SUB + HALO, :]
        before = jnp.where(pos0 > 0, before, 0.0)
        after = jnp.where(pos0 + SUB < seq_len, after, 0.0)
        cur = pcur_ref[body, :]
        ext = jnp.concatenate([before, cur, after], axis=0)
        pos = pos0 + lax.broadcasted_iota(jnp.int32, (SUB, group_width), 0)
        pieces = [attn_ref[body, :]]
        for g, w in enumerate(POOL_WINDOWS):
            lanes = slice(g * group_width, (g + 1) * group_width)
            total = _window_sum(ext[:, lanes], w)[HALO:HALO + SUB]
            count = jnp.minimum(pos + w // 2, seq_len) - jnp.maximum(pos - w // 2, 0)
            pooled = total / count.astype(F32) - cur[:, lanes]
            mixed = jnp.dot(pooled.astype(BF16), wpool_ref[g], preferred_element_type=F32)
            pieces.append((mixed * pscale_ref[:, lanes]).astype(BF16))
        mix = jnp.dot(jnp.concatenate(pieces, axis=-1), wout_ref[...], preferred_element_type=F32)
        x1 = _deepnorm(x_ref[body, :], gate1, mix) * g1_ref[...] + b1_ref[...]
        x1_ref[body, :] = x1
        u2_ref[body, :] = (_layer_norm(x1) * (1.0 + scale2) + shift2).astype(BF16)


def _mix(attn2d, p2d, x2d, mod, mod_row, w_pool, pool_scale, w_out, ln_g, ln_b, seq_len, tm):
    n_tok, d = x2d.shape
    pool_width = p2d.shape[1]
    assert tm % SUB == 0 and seq_len % SUB == 0 and (seq_len % tm == 0 or tm % seq_len == 0)
    halo_per_block = tm // HALO
    n_halo_blocks = n_tok // HALO
    return pl.pallas_call(
        functools.partial(_mix_kernel, tm=tm, seq_len=seq_len),
        grid=(n_tok // tm,),
        in_specs=[
            pl.BlockSpec((tm, ATTN_WIDTH), lambda i: (i, 0)),
            pl.BlockSpec((HALO, pool_width), lambda i: (jnp.maximum(i * halo_per_block - 1, 0), 0)),
            pl.BlockSpec((tm, pool_width), lambda i: (i, 0)),
            pl.BlockSpec((HALO, pool_width),
                         lambda i: (jnp.minimum((i + 1) * halo_per_block, n_halo_blocks - 1), 0)),
            pl.BlockSpec((tm, d), lambda i: (i, 0)),
            pl.BlockSpec((1, N_MOD, d), lambda i: (mod_row(i, tm), 0, 0)),
            _resident(w_pool.shape),
            _resident((1, pool_width)),
            _resident(w_out.shape),
            _resident((1, d)),
            _resident((1, d)),
        ],
        out_specs=[
            pl.BlockSpec((tm, d), lambda i: (i, 0)),
            pl.BlockSpec((tm, d), lambda i: (i, 0)),
        ],
        out_shape=[
            jax.ShapeDtypeStruct((n_tok, d), F32),
            jax.ShapeDtypeStruct((n_tok, d), BF16),
        ],
        compiler_params=_params(("arbitrary",)),
        name="mix",
    )(attn2d, p2d, p2d, p2d, x2d, mod, w_pool, pool_scale, w_out, ln_g, ln_b)


def _mlp_kernel(u_ref, x1_ref, mod_ref, w1_ref, w2_ref, g2_ref, b2_ref, o_ref, acc_ref):
    j = pl.program_id(1)

    @pl.when(j == 0)
    def _():
        acc_ref[...] = jnp.zeros_like(acc_ref)

    h = jnp.maximum(jnp.dot(u_ref[...], w1_ref[...], preferred_element_type=F32), 0.0)
    acc_ref[...] += jnp.dot((h * h).astype(BF16), w2_ref[...], preferred_element_type=F32)

    @pl.when(j == pl.num_programs(1) - 1)
    def _():
        gate2 = mod_ref[0, 5:6, :]
        o_ref[...] = _deepnorm(x1_ref[...], gate2, acc_ref[...]) * g2_ref[...] + b2_ref[...]


def _mlp(u2, x1, mod, mod_row, w1, w2, ln_g, ln_b, tm, tf):
    n_tok, d = x1.shape
    d_ff = w1.shape[1]
    return pl.pallas_call(
        _mlp_kernel,
        grid=(n_tok // tm, d_ff // tf),
        in_specs=[
            pl.BlockSpec((tm, d), lambda i, j: (i, 0)),
            pl.BlockSpec((tm, d), lambda i, j: (i, 0)),
            pl.BlockSpec((1, N_MOD, d), lambda i, j: (mod_row(i, tm), 0, 0)),
            pl.BlockSpec((d, tf), lambda i, j: (0, j)),
            pl.BlockSpec((tf, d), lambda i, j: (j, 0)),
            _resident((1, d)),
            _resident((1, d)),
        ],
        out_specs=pl.BlockSpec((tm, d), lambda i, j: (i, 0)),
        out_shape=jax.ShapeDtypeStruct((n_tok, d), F32),
        scratch_shapes=[pltpu.VMEM((tm, d), F32)],
        compiler_params=_params(("arbitrary", "arbitrary")),
        name="mlp",
    )(u2, x1, mod, w1, w2, ln_g, ln_b)


def _layer(x, mod, mod_row, weights, rope_tables, cache, tm_proj, tq, tm_mix, tm_mlp, tf):
    (w_in, q_gain, k_gain, w_pool, pool_scale, w_out, ln1_g, ln1_b, w_ff1, w_ff2, ln2_g, ln2_b) = weights
    b, t, d = x.shape
    x2d = x.reshape(b * t, d)
    kv_dtype = F32 if cache is None else BF16
    q, k, v, p = _projection(x2d, mod, mod_row, w_in, q_gain, k_gain, rope_tables, t, tm_proj, kv_dtype)
    k3 = k.reshape(b, t, KV_WIDTH)
    v3 = v.reshape(b, t, KV_WIDTH)
    segments = ([] if cache is None else [cache]) + [(k3, v3)]
    q_bound = (HEAD_DIM * Q_SCALE ** 2 * NORM_MARGIN * jnp.max(q_gain * q_gain)).reshape(1)
    attn = _attention(q.reshape(b, t, ATTN_WIDTH), q_bound, segments, tq)
    x1, u2 = _mix(attn.reshape(b * t, ATTN_WIDTH), p, x2d, mod, mod_row, w_pool, pool_scale, w_out,
                  ln1_g, ln1_b, t, tm_mix)
    y = _mlp(u2, x1, mod, mod_row, w_ff1, w_ff2, ln2_g, ln2_b, tm_mlp, tf)
    return y.reshape(b, t, d), k3, v3


def kernel(x_prompt, x_sample, cache_k, cache_v, c, c_ctx, w_mod, b_mod, w_in, q_gain, k_gain, w_pool,
           pool_scale, w_out, ln1_g, ln1_b, w_ff1, w_ff2, ln2_g, ln2_b):
    assert w_mod.shape[0] == DEPTH and cache_k.shape[1] == DEPTH
    n_ctx, t_ctx, d = x_prompt.shape
    n_lat, t_lat, _ = x_sample.shape
    past = cache_k.shape[2]

    n_rows = -(-(1 + n_lat) // (2 * SUBLANES)) * (2 * SUBLANES)
    cond = jnp.concatenate([c_ctx[None], c, jnp.zeros((n_rows - 1 - n_lat, d), F32)], axis=0)
    mod = _modulation(cond, w_mod[0], b_mod[0]).reshape(n_rows, N_MOD, d)

    row2 = lambda a: a[0].reshape(1, -1)
    weights = (w_in[0].astype(BF16), row2(q_gain), row2(k_gain), w_pool[0].astype(BF16), row2(pool_scale),
               w_out[0].astype(BF16), row2(ln1_g), row2(ln1_b), w_ff1[0].astype(BF16), w_ff2[0].astype(BF16),
               row2(ln2_g), row2(ln2_b))

    y_ctx, k_ctx, v_ctx = _layer(
        x_prompt, mod, lambda i, tm: 0, weights, None, None,
        tm_proj=512, tq=t_ctx, tm_mix=512, tm_mlp=512, tf=1024)

    cache = (cache_k.reshape(n_lat, past, KV_WIDTH), cache_v.reshape(n_lat, past, KV_WIDTH))
    y_lat, _, _ = _layer(
        x_sample, mod, lambda i, tm: 1 + i // (t_lat // tm), weights, _rope_tables(t_lat), cache,
        tm_proj=512, tq=512, tm_mix=512, tm_mlp=512, tf=1024)

    ctx_k = k_ctx.reshape(n_ctx, 1, t_ctx, N_KV_HEADS, HEAD_DIM)
    ctx_v = v_ctx.reshape(n_ctx, 1, t_ctx, N_KV_HEADS, HEAD_DIM)
    return (y_ctx, y_lat, ctx_k, ctx_v)
```
